```python
import jax, jax.numpy as jnp
from jax import lax
import numpy as np

D_MODEL = 1024
BATCH = 1
SEQ = 16384
DEPTH = 4
DEC_BATCH = 16
DEC_SEQ = 64
PAST_LEN = 4096

CHUNK = 64
QBLOCK = 128
EPS = 1e-6
GN_EPS = 64e-5

MLA_HEADS = 8
MLA_NOPE = 64
MLA_ROPE = 32
MLA_QK = MLA_NOPE + MLA_ROPE
MLA_V = 64
MLA_WIDTH = MLA_HEADS * MLA_V
Q_RANK = 256
KV_RANK = 128
ROPE_THETA = 10000.0

RW_HEADS = 8
RW_HEAD = 64
RW_WIDTH = RW_HEADS * RW_HEAD
DECAY_LORA = 64
ICLR_LORA = 64
SHIFT_WIDTH = 3 * RW_WIDTH + DECAY_LORA + ICLR_LORA

O_CQ = Q_RANK
O_CKV = O_CQ + KV_RANK
O_KPE = O_CKV + MLA_ROPE
O_GA = O_KPE + MLA_WIDTH
O_SH = O_GA + SHIFT_WIDTH
O_GB = O_SH + RW_WIDTH
O_MA = O_GB + D_MODEL
N_IN = O_MA + D_MODEL
IN_SPLITS = (O_CQ, O_CKV, O_KPE, O_GA, O_SH, O_GB, O_MA)
SHIFT_SPLITS = (RW_WIDTH, 2 * RW_WIDTH, 3 * RW_WIDTH, 3 * RW_WIDTH + DECAY_LORA)

kernel_name = 'mla_rwkv7_gated_parallel_stream_step'


def _rmsnorm(x, g):
    xf = x.astype(jnp.float32)
    y = xf * lax.rsqrt(jnp.mean(xf * xf, axis=-1, keepdims=True) + EPS)
    return (y * g.astype(jnp.float32)).astype(x.dtype)


def _rope(x, pos):
    half = MLA_ROPE // 2
    inv = ROPE_THETA ** (-jnp.arange(half, dtype=jnp.float32) / half)
    ang = pos.astype(jnp.float32)[:, None] * inv[None, :]
    shape = (1, pos.shape[0]) + (1,) * (x.ndim - 3) + (half,)
    cos = jnp.cos(ang).reshape(shape)
    sin = jnp.sin(ang).reshape(shape)
    xf = x.astype(jnp.float32)
    x1, x2 = xf[..., :half], xf[..., half:]
    return jnp.concatenate([x1 * cos - x2 * sin, x2 * cos + x1 * sin], axis=-1).astype(x.dtype)


def _chunk_attention(q_nope, q_pe, k_nope, k_pe, v, q_pos, k_pos):
    B, Tq, H, _ = q_nope.shape
    scale = MLA_QK ** -0.5
    kn = k_nope.astype(jnp.float32)
    kr = k_pe.astype(jnp.float32)
    vf = v.astype(jnp.float32)
    k_chunk = k_pos // CHUNK

    def block(args):
        qn, qr, qp = args
        s = (jnp.einsum('bqhd,bkhd->bhqk', qn.astype(jnp.float32), kn)
             + jnp.einsum('bqhr,bkr->bhqk', qr.astype(jnp.float32), kr)) * scale
        allowed = k_chunk[None, :] <= (qp // CHUNK)[:, None]
        s = jnp.where(allowed[None, None], s, -jnp.inf)
        p = jax.nn.softmax(s, axis=-1)
        return jnp.einsum('bhqk,bkhd->bqhd', p, vf)

    if Tq > QBLOCK:
        nb = Tq // QBLOCK
        qn_b = jnp.moveaxis(q_nope.reshape(B, nb, QBLOCK, H, MLA_NOPE), 1, 0)
        qr_b = jnp.moveaxis(q_pe.reshape(B, nb, QBLOCK, H, MLA_ROPE), 1, 0)
        o = lax.map(block, (qn_b, qr_b, q_pos.reshape(nb, QBLOCK)))
        o = jnp.moveaxis(o, 0, 1).reshape(B, Tq, H, MLA_V)
    else:
        o = block((q_nope, q_pe, q_pos))
    return o.astype(q_nope.dtype)


def _wkv7(r, w, k, v, a, b, s0):
    def step(s, inp):
        rt, wt, kt, vt, at, bt = inp
        sa = jnp.einsum('bhvk,bhk->bhv', s, at)
        s = s * wt[:, :, None, :] + sa[..., None] * bt[:, :, None, :] + vt[..., None] * kt[:, :, None, :]
        return s, jnp.einsum('bhvk,bhk->bhv', s, rt)
    xs = (jnp.moveaxis(r, 1, 0), jnp.moveaxis(w, 1, 0), jnp.moveaxis(k, 1, 0),
          jnp.moveaxis(v, 1, 0), jnp.moveaxis(a, 1, 0), jnp.moveaxis(b, 1, 0))
    s_final, ys = lax.scan(step, s0, xs)
    return jnp.moveaxis(ys, 0, 1), s_final


def _l2norm_heads(x):
    B, T, _ = x.shape
    xh = x.reshape(B, T, RW_HEADS, RW_HEAD)
    n = jnp.sqrt(jnp.sum(xh * xh, axis=-1, keepdims=True))
    return (xh / jnp.maximum(n, 1e-12)).reshape(B, T, RW_WIDTH)


def _head_groupnorm(y, g, bias):
    B, T = y.shape[0], y.shape[1]
    mu = jnp.mean(y, axis=-1, keepdims=True)
    var = jnp.mean(jnp.square(y - mu), axis=-1, keepdims=True)
    yn = ((y - mu) * lax.rsqrt(var + GN_EPS)).reshape(B, T, RW_WIDTH)
    return yn * g.astype(jnp.float32) + bias.astype(jnp.float32)


def _mixer_layer(x, ckv_past, kpe_past, wkv0, shift0, lw):
    (norm_w, w_in, q_norm_w, kv_norm_w, w_uq, w_ukv, qn_nope, qn_rope, kn_nope, kn_rope,
     mu_shift, w0, w2, a0, a2, k_k, k_a, r_k, lnx_w, lnx_b, w_out_a, w_out_b, w_o) = lw
    B, T, _ = x.shape
    P = ckv_past.shape[1]
    dt = x.dtype
    f32 = jnp.float32
    q_pos = P + jnp.arange(T, dtype=jnp.int32)
    k_pos = jnp.arange(P + T, dtype=jnp.int32)

    h = _rmsnorm(x, norm_w)
    z = h @ w_in
    c_q, kv_lat, k_pe, gate_a, zs, gate_b, merge_a, merge_b = jnp.split(z, IN_SPLITS, axis=-1)

    q = (_rmsnorm(c_q, q_norm_w) @ w_uq).reshape(B, T, MLA_HEADS, MLA_QK)
    q_nope = _rmsnorm(q[..., :MLA_NOPE], qn_nope)
    q_pe = _rope(_rmsnorm(q[..., MLA_NOPE:], qn_rope), q_pos)
    c_kv = _rmsnorm(kv_lat, kv_norm_w)
    k_pe = _rope(_rmsnorm(k_pe, kn_rope), q_pos)
    ckv_all = jnp.concatenate([ckv_past.astype(dt), c_kv], axis=1)
    kpe_all = jnp.concatenate([kpe_past.astype(dt), k_pe], axis=1)
    kv = (ckv_all @ w_ukv).reshape(B, P + T, MLA_HEADS, MLA_NOPE + MLA_V)
    k_nope = _rmsnorm(kv[..., :MLA_NOPE], kn_nope)
    v_a = kv[..., MLA_NOPE:]
    o_a = _chunk_attention(q_nope, q_pe, k_nope, kpe_all, v_a, q_pos, k_pos)
    o_a = o_a.reshape(B, T, MLA_WIDTH) * jax.nn.silu(gate_a)

    zs_prev = jnp.concatenate([shift0.astype(dt), zs[:, :-1]], axis=1)
    zm = (zs + (zs_prev - zs) * mu_shift).astype(f32)
    r, k, v, wd, ad = jnp.split(zm, SHIFT_SPLITS, axis=-1)
    log_w = -jax.nn.softplus(-(w0 + jnp.tanh(wd) @ w2)) - 0.5
    decay = jnp.exp(-jnp.exp(log_w.astype(f32)))
    a = jax.nn.sigmoid(a0 + ad @ a2).astype(f32)
    kk = _l2norm_heads(k * k_k)
    k = k * (1.0 + (a - 1.0) * k_a)
    hs = lambda t: t.astype(f32).reshape(B, T, RW_HEADS, RW_HEAD)
    y, wkv_final = _wkv7(hs(r), hs(decay), hs(k), hs(v), hs(-kk), hs(kk * a), wkv0.astype(f32))
    y = _head_groupnorm(y, lnx_w, lnx_b)
    bonus = jnp.sum(hs(r) * hs(k) * r_k.astype(f32), axis=-1, keepdims=True) * hs(v)
    o_b = (y + bonus.reshape(B, T, RW_WIDTH)).astype(dt) * jax.nn.silu(gate_b)

    m = jax.nn.sigmoid(merge_a) * (o_a @ w_out_a) + jax.nn.sigmoid(merge_b) * (o_b @ w_out_b)
    out = x + m @ w_o
    return out, c_kv, k_pe, wkv_final.astype(dt), zs[:, -1:]


def setup_inputs(seed: int = 0) -> dict:
    key = jax.random.key(seed)
    ks = jax.random.split(key, 32)
    f = jnp.float32
    n = jax.random.normal

    def gain(k, shape):
        return 1.0 + 0.02 * n(k, shape, f)

    return {
        'x_prompt': n(ks[0], (BATCH, SEQ, D_MODEL), f),
        'x_sample': n(ks[1], (DEC_BATCH, DEC_SEQ, D_MODEL), f),
        'cache_ckv': n(ks[2], (DEPTH, DEC_BATCH, PAST_LEN, KV_RANK), f),
        'cache_kpe': n(ks[3], (DEPTH, DEC_BATCH, PAST_LEN, MLA_ROPE), f),
        'state_wkv': 0.5 * n(ks[4], (DEPTH, DEC_BATCH, RW_HEADS, RW_HEAD, RW_HEAD), f),
        'state_shift': n(ks[5], (DEPTH, DEC_BATCH, 1, SHIFT_WIDTH), f),
        'norm_w': gain(ks[6], (DEPTH, D_MODEL)),
        'w_in': n(ks[7], (DEPTH, D_MODEL, N_IN), f) * D_MODEL ** -0.5,
        'q_norm_w': gain(ks[8], (DEPTH, Q_RANK)),
        'kv_norm_w': gain(ks[9], (DEPTH, KV_RANK)),
        'w_uq': n(ks[10], (DEPTH, Q_RANK, MLA_HEADS * MLA_QK), f) * Q_RANK ** -0.5,
        'w_ukv': n(ks[11], (DEPTH, KV_RANK, MLA_HEADS * (MLA_NOPE + MLA_V)), f) * KV_RANK ** -0.5,
        'qn_nope': gain(ks[12], (DEPTH, MLA_NOPE)),
        'qn_rope': gain(ks[13], (DEPTH, MLA_ROPE)),
        'kn_nope': gain(ks[14], (DEPTH, MLA_NOPE)),
        'kn_rope': gain(ks[15], (DEPTH, MLA_ROPE)),
        'mu_shift': jax.random.uniform(ks[16], (DEPTH, SHIFT_WIDTH), f),
        'w0': jax.random.uniform(ks[17], (DEPTH, RW_WIDTH), f, -6.0, 1.0),
        'w2': 0.5 * n(ks[18], (DEPTH, DECAY_LORA, RW_WIDTH), f) * DECAY_LORA ** -0.5,
        'a0': 0.5 * n(ks[19], (DEPTH, RW_WIDTH), f),
        'a2': 0.5 * n(ks[20], (DEPTH, ICLR_LORA, RW_WIDTH), f) * ICLR_LORA ** -0.5,
        'k_k': 1.0 + 0.1 * n(ks[21], (DEPTH, RW_WIDTH), f),
        'k_a': 1.0 + 0.1 * n(ks[22], (DEPTH, RW_WIDTH), f),
        'r_k': 0.1 * n(ks[23], (DEPTH, RW_HEADS, RW_HEAD), f),
        'lnx_w': gain(ks[24], (DEPTH, RW_WIDTH)),
        'lnx_b': 0.02 * n(ks[25], (DEPTH, RW_WIDTH), f),
        'w_out_a': n(ks[26], (DEPTH, MLA_WIDTH, D_MODEL), f) * MLA_WIDTH ** -0.5,
        'w_out_b': n(ks[27], (DEPTH, RW_WIDTH, D_MODEL), f) * RW_WIDTH ** -0.5,
        'w_o': 0.5 * n(ks[28], (DEPTH, D_MODEL, D_MODEL), f) * D_MODEL ** -0.5,
    }


def reference(x_prompt, x_sample, cache_ckv, cache_kpe, state_wkv, state_shift,
              norm_w, w_in, q_norm_w, kv_norm_w, w_uq, w_ukv, qn_nope, qn_rope, kn_nope, kn_rope,
              mu_shift, w0, w2, a0, a2, k_k, k_a, r_k, lnx_w, lnx_b, w_out_a, w_out_b, w_o):
    B = x_prompt.shape[0]
    dt = x_prompt.dtype
    empty_ckv = jnp.zeros((B, 0, KV_RANK), dt)
    empty_kpe = jnp.zeros((B, 0, MLA_ROPE), dt)
    zero_wkv = jnp.zeros((B, RW_HEADS, RW_HEAD, RW_HEAD), jnp.float32)
    zero_shift = jnp.zeros((B, 1, SHIFT_WIDTH), dt)

    yp, ys = x_prompt, x_sample
    ckv_p, kpe_p, wkv_p, sh_p = [], [], [], []
    ckv_s, kpe_s, wkv_s, sh_s = [], [], [], []
    for i in range(DEPTH):
        lw = (norm_w[i], w_in[i], q_norm_w[i], kv_norm_w[i], w_uq[i], w_ukv[i], qn_nope[i], qn_rope[i],
              kn_nope[i], kn_rope[i], mu_shift[i], w0[i], w2[i], a0[i], a2[i], k_k[i], k_a[i], r_k[i],
              lnx_w[i], lnx_b[i], w_out_a[i], w_out_b[i], w_o[i])
        yp, c1, c2, c3, c4 = _mixer_layer(yp, empty_ckv, empty_kpe, zero_wkv, zero_shift, lw)
        ckv_p.append(c1); kpe_p.append(c2); wkv_p.append(c3); sh_p.append(c4)
        ys, d1, d2, d3, d4 = _mixer_layer(ys, cache_ckv[i], cache_kpe[i], state_wkv[i], state_shift[i], lw)
        ckv_s.append(d1); kpe_s.append(d2); wkv_s.append(d3); sh_s.append(d4)

    return (yp, ys,
            jnp.stack(ckv_p), jnp.stack(kpe_p), jnp.stack(wkv_p), jnp.stack(sh_p),
            jnp.stack(ckv_s), jnp.stack(kpe_s), jnp.stack(wkv_s), jnp.stack(sh_s))
```

```python
import functools

import jax
import jax.numpy as jnp
from jax import lax
from jax.experimental import pallas as pl
from jax.experimental.pallas import tpu as pltpu

F32 = jnp.float32
BF16 = jnp.bfloat16
HI = lax.Precision.HIGHEST

D_MODEL = 1024
CHUNK = 64
CHUNK_SHIFT = 6
EPS = 1e-6
GN_EPS = 64e-5
HEADS = 8
NOPE = 64
ROPE = 32
QK = NOPE + ROPE
Q_RANK = 256
KV_RANK = 128
ROPE_THETA = 10000.0
RW_WIDTH = 512
RW_HEAD = 64
LORA = 64
SHIFT_WIDTH = 3 * RW_WIDTH + 2 * LORA
LANES = 128
HEAD_PAIRS = HEADS // 2
NEG = -1e30

C_LAT = 0
C_GA = 640
C_ZS = C_GA + 512
C_GB = C_ZS + SHIFT_WIDTH
C_MA = C_GB + 512
C_MB = C_MA + D_MODEL
C_END = C_MB + D_MODEL

VMEM_LIMIT = 56 * 1024 * 1024


def _const_spec(shape):
    n = len(shape)
    return pl.BlockSpec(shape, lambda *_: (0,) * n)


def _dot(a, b):
    return jnp.dot(a, b, preferred_element_type=F32)


def _dot_hi(a, b):
    return jnp.dot(a, b, preferred_element_type=F32, precision=HI)


def _dot_nt(a, b, precision=None):
    return lax.dot_general(a, b, (((1,), (1,)), ((), ())), preferred_element_type=F32, precision=precision)


def _dot_tn(a, b, precision=None):
    return lax.dot_general(a, b, (((0,), (0,)), ((), ())), preferred_element_type=F32, precision=precision)


def _seg_sum(x, seg):
    hi = x.astype(BF16)
    lo = (x - hi.astype(F32)).astype(BF16)
    return _dot(hi, seg) + _dot(lo, seg)


def _in_kernel(x_ref, tab_ref, sh0_ref, normw_ref, win_ref, qnw_ref, wuq_ref, wuqs_ref, g128_ref, mu_ref,
               v512_ref, w2_ref, a2_ref, seg_ref,
               q_ref, ckv_ref, kpe_ref, rw_ref, bon_ref, ga_ref, gb_ref, ma_ref, mb_ref, shift_ref,
               carry_ref, *, tm, seq, scale):
    i = pl.program_id(0)
    x = x_ref[...]
    h = x * lax.rsqrt(jnp.mean(x * x, axis=-1, keepdims=True) + EPS) * normw_ref[...]
    hb = h.astype(BF16)
    tab = tab_ref[...]
    g128 = g128_ref[...]

    z1 = _dot(hb, win_ref[:, C_LAT:C_GA])
    cq = z1[:, :Q_RANK]
    cqn = cq * lax.rsqrt(jnp.mean(cq * cq, axis=-1, keepdims=True) + EPS) * qnw_ref[...]
    kvl = z1[:, Q_RANK:Q_RANK + KV_RANK]
    ckv_ref[...] = kvl * lax.rsqrt(jnp.mean(kvl * kvl, axis=-1, keepdims=True) + EPS) * g128[4:5]
    t1 = z1[:, 384:512]
    t2 = z1[:, 512:640]
    kinv = lax.rsqrt(jnp.sum(t1 * t1, axis=-1, keepdims=True) * (1.0 / ROPE) + EPS)
    kr = (t1 * g128[2:3] * tab[:, 256:384] + t2 * g128[3:4] * tab[:, 384:512]) * kinv
    kpe_ref[...] = kr[:, :ROPE]

    cqb = cqn.astype(BF16)
    qr = _dot(cqb, wuq_ref[...])
    qs = _dot(cqb, wuqs_ref[...])
    lane = lax.broadcasted_iota(jnp.int32, (1, LANES), 1)
    is_nope = lane < NOPE
    cos_f = tab[:, 0:128]
    sin_f = tab[:, 128:256]
    for hd in range(HEADS):
        t = qr[:, hd * LANES:(hd + 1) * LANES]
        sq = t * t
        ms_n = jnp.sum(jnp.where(is_nope, sq, 0.0), axis=-1, keepdims=True) * (1.0 / NOPE)
        ms_r = jnp.sum(jnp.where(is_nope, 0.0, sq), axis=-1, keepdims=True) * (1.0 / ROPE)
        inv = jnp.where(is_nope, lax.rsqrt(ms_n + EPS), lax.rsqrt(ms_r + EPS))
        qh = (t * g128[0:1] * cos_f + qs[:, hd * LANES:(hd + 1) * LANES] * g128[1:2] * sin_f) * (inv * scale)
        q_ref[hd] = qh.astype(BF16)

    ga_ref[...] = jax.nn.silu(_dot(hb, win_ref[:, C_GA:C_ZS]))
    gb_ref[...] = jax.nn.silu(_dot(hb, win_ref[:, C_GB:C_MA]))
    ma_ref[...] = jax.nn.sigmoid(_dot(hb, win_ref[:, C_MA:C_MB]))
    mb_ref[...] = jax.nn.sigmoid(_dot(hb, win_ref[:, C_MB:C_END]))

    zs = _dot(hb, win_ref[:, C_ZS:C_GB])
    rows = lax.broadcasted_iota(jnp.int32, (tm, 1), 0)
    prev = pltpu.roll(zs, 1, axis=0)
    if seq >= tm:
        tiles_per_seq = seq // tm
        first = jnp.where(i % tiles_per_seq == 0, sh0_ref[0], carry_ref[...])
        prev = jnp.where(rows == 0, first, prev)
        carry_ref[...] = zs[tm - 1:tm, :]
        shift_ref[0] = zs[tm - 1:tm, :]
    else:
        for s in range(tm // seq):
            prev = jnp.where(rows == s * seq, sh0_ref[0, s:s + 1, :], prev)
            shift_ref[0, s:s + 1, :] = zs[(s + 1) * seq - 1:(s + 1) * seq, :]
    zm = zs + (prev - zs) * mu_ref[...]
    r = zm[:, 0:512]
    k = zm[:, 512:1024]
    v = zm[:, 1024:1536]
    t13 = zm[:, 1536:1664]
    v512 = v512_ref[...]
    w0, a0, k_k, k_a, r_k = v512[0:1], v512[1:2], v512[2:3], v512[3:4], v512[4:5]
    seg = seg_ref[...]
    log_w = -jax.nn.softplus(-(w0 + _dot_hi(jnp.tanh(t13), w2_ref[...]))) - 0.5
    lw = -jnp.exp(log_w)
    aic = jax.nn.sigmoid(a0 + _dot_hi(t13, a2_ref[...]))
    kkr = k * k_k
    kk = kkr / jnp.maximum(jnp.sqrt(_seg_sum(kkr * kkr, seg)), 1e-12)
    k2 = k * (1.0 + (aic - 1.0) * k_a)
    bon_ref[...] = _seg_sum(r * k2 * r_k, seg) * v
    rw_ref[:, 0:512] = r
    rw_ref[:, 512:1024] = lw
    rw_ref[:, 1024:1536] = k2
    rw_ref[:, 1536:2048] = v
    rw_ref[:, 2048:2560] = -kk
    rw_ref[:, 2560:3072] = kk * aic


def _in_stage(x2d, tab, sh0, lw, *, seq):
    n = x2d.shape[0]
    tm = 256 if seq >= 256 else min(256, n)
    assert n % tm == 0 and (seq % tm == 0 or tm % seq == 0)
    nseq = max(1, tm // seq)
    batch = n // seq
    sh0 = sh0.reshape(batch // nseq, nseq, SHIFT_WIDTH)
    if seq >= tm:
        sh_map = lambda i: (i // (seq // tm), 0, 0)
    else:
        sh_map = lambda i: (i, 0, 0)
    row = lambda w: pl.BlockSpec((tm, w), lambda i: (i, 0))
    out_shape = (
        jax.ShapeDtypeStruct((HEADS, n, LANES), BF16),
        jax.ShapeDtypeStruct((n, KV_RANK), F32),
        jax.ShapeDtypeStruct((n, ROPE), F32),
        jax.ShapeDtypeStruct((n, 6 * RW_WIDTH), F32),
        jax.ShapeDtypeStruct((n, RW_WIDTH), F32),
        jax.ShapeDtypeStruct((n, 512), F32),
        jax.ShapeDtypeStruct((n, 512), F32),
        jax.ShapeDtypeStruct((n, D_MODEL), F32),
        jax.ShapeDtypeStruct((n, D_MODEL), F32),
        jax.ShapeDtypeStruct((batch // nseq, nseq, SHIFT_WIDTH), F32),
    )
    out_specs = (
        pl.BlockSpec((HEADS, tm, LANES), lambda i: (0, i, 0)),
        row(KV_RANK), row(ROPE), row(6 * RW_WIDTH), row(RW_WIDTH), row(512), row(512), row(D_MODEL), row(D_MODEL),
        pl.BlockSpec((1, nseq, SHIFT_WIDTH), sh_map),
    )
    in_specs = [
        row(D_MODEL), row(512), pl.BlockSpec((1, nseq, SHIFT_WIDTH), sh_map),
        _const_spec((1, D_MODEL)),
        pl.BlockSpec((D_MODEL, C_END), lambda i: (0, 0), pipeline_mode=pl.Buffered(1)),
        _const_spec((1, Q_RANK)), _const_spec((Q_RANK, HEADS * LANES)), _const_spec((Q_RANK, HEADS * LANES)),
        _const_spec((8, LANES)), _const_spec((1, SHIFT_WIDTH)), _const_spec((8, RW_WIDTH)),
        _const_spec((LANES, RW_WIDTH)), _const_spec((LANES, RW_WIDTH)), _const_spec((RW_WIDTH, RW_WIDTH)),
    ]
    return pl.pallas_call(
        functools.partial(_in_kernel, tm=tm, seq=seq, scale=QK ** -0.5),
        grid=(n // tm,),
        in_specs=in_specs,
        out_specs=out_specs,
        out_shape=out_shape,
        scratch_shapes=[pltpu.VMEM((1, SHIFT_WIDTH), F32)],
        compiler_params=pltpu.CompilerParams(dimension_semantics=("arbitrary",), vmem_limit_bytes=VMEM_LIMIT),
        name="in_stage",
    )(x2d, tab, sh0, lw["norm_w"], lw["w_in"], lw["q_norm_w"], lw["w_uq"], lw["w_uq_sw"], lw["g128"], lw["mu"],
      lw["v512"], lw["w2"], lw["a2"], lw["seg"])


def _kv_kernel(ckv_ref, kpe_ref, wukv_ref, g128_ref, one_ref, k_ref, v_ref):
    kv = _dot(ckv_ref[...].astype(BF16), wukv_ref[...])
    kpe = kpe_ref[...]
    gkn = g128_ref[5:6]
    for hd in range(HEADS):
        kr = kv[:, hd * LANES:(hd + 1) * LANES]
        inv = lax.rsqrt(jnp.sum(kr * kr, axis=-1, keepdims=True) * (1.0 / NOPE) + EPS)
        k_ref[hd] = (kr * inv * gkn + kpe).astype(BF16)
        vr = kv[:, (HEADS + hd) * LANES:(HEADS + hd + 1) * LANES]
        v_ref[hd] = (vr + one_ref[hd % 2:hd % 2 + 1]).astype(BF16)


def _kv_stage(ckv2d, kpe2d, lw, *, tile):
    n = ckv2d.shape[0]
    assert n % tile == 0
    row = pl.BlockSpec((tile, LANES), lambda i: (i, 0))
    hspec = pl.BlockSpec((HEADS, tile, LANES), lambda i: (0, i, 0))
    return pl.pallas_call(
        _kv_kernel,
        grid=(n // tile,),
        in_specs=[row, row, _const_spec((KV_RANK, 2 * HEADS * LANES)), _const_spec((8, LANES)),
                  _const_spec((2, LANES))],
        out_specs=(hspec, hspec),
        out_shape=(jax.ShapeDtypeStruct((HEADS, n, LANES), BF16),) * 2,
        compiler_params=pltpu.CompilerParams(dimension_semantics=("parallel",), vmem_limit_bytes=VMEM_LIMIT),
        name="kv_stage",
    )(ckv2d, kpe2d, lw["w_ukv"], lw["g128"], lw["onecol"])


def _attn_kernel(q_ref, k_ref, v_ref, o_ref, m_scr, acc_scr, *, tq, tk, past, kv_len, nk):
    qi = pl.program_id(1)
    kj = pl.program_id(2)

    @pl.when(kj == 0)
    def _init():
        m_scr[...] = jnp.full(m_scr.shape, NEG, F32)
        acc_scr[...] = jnp.zeros(acc_scr.shape, F32)

    q_first = past + qi * tq
    q_last = q_first + tq - 1
    k_first = kj * tk
    k_last = k_first + tk - 1
    needed = (k_first >> CHUNK_SHIFT) <= (q_last >> CHUNK_SHIFT)
    unmasked = jnp.logical_and((k_last >> CHUNK_SHIFT) <= (q_first >> CHUNK_SHIFT), k_last < kv_len)

    def compute(masked):
        if masked:
            rows = q_first + lax.broadcasted_iota(jnp.int32, (tq, tk), 0)
            cols = k_first + lax.broadcasted_iota(jnp.int32, (tq, tk), 1)
            ok = jnp.logical_and((cols >> CHUNK_SHIFT) <= (rows >> CHUNK_SHIFT), cols < kv_len)
        for hd in range(HEADS):
            s = _dot_nt(q_ref[hd], k_ref[hd])
            if masked:
                s = jnp.where(ok, s, NEG)
            m_prev = m_scr[hd]
            m_new = jnp.maximum(m_prev, jnp.max(s, axis=-1, keepdims=True))
            alpha = jnp.exp(m_prev - m_new)
            p = jnp.exp(s - m_new[:, 0:1])
            acc_scr[hd] = acc_scr[hd] * alpha + _dot(p.astype(BF16), v_ref[hd])
            m_scr[hd] = m_new

    @pl.when(jnp.logical_and(needed, unmasked))
    def _full():
        compute(False)

    @pl.when(jnp.logical_and(needed, jnp.logical_not(unmasked)))
    def _diag():
        compute(True)

    @pl.when(kj == nk - 1)
    def _finish():
        lane = lax.broadcasted_iota(jnp.int32, (1, LANES), 1)
        for p in range(HEAD_PAIRS):
            ae = acc_scr[2 * p]
            ao = acc_scr[2 * p + 1]
            o_ref[:, p * LANES:(p + 1) * LANES] = jnp.where(lane < 64, ae / ae[:, 64:65], ao / ao[:, 0:1])


def _attn_stage(q, k, v, *, batch, seq, past, kv_len, kv_pad, tq, tk):
    nq = seq // tq
    nk = kv_pad // tk
    assert seq % tq == 0 and kv_pad % tk == 0

    def kv_map(b, qi, kj):
        last = jnp.minimum((past + qi * tq + tq - 1) // tk, nk - 1)
        return (0, b * nk + jnp.minimum(kj, last), 0)

    return pl.pallas_call(
        functools.partial(_attn_kernel, tq=tq, tk=tk, past=past, kv_len=kv_len, nk=nk),
        grid=(batch, nq, nk),
        in_specs=[pl.BlockSpec((HEADS, tq, LANES), lambda b, qi, kj: (0, b * nq + qi, 0)),
                  pl.BlockSpec((HEADS, tk, LANES), kv_map),
                  pl.BlockSpec((HEADS, tk, LANES), kv_map)],
        out_specs=pl.BlockSpec((tq, HEADS * NOPE), lambda b, qi, kj: (b * nq + qi, 0)),
        out_shape=jax.ShapeDtypeStruct((batch * seq, HEADS * NOPE), F32),
        scratch_shapes=[pltpu.VMEM((HEADS, tq, LANES), F32), pltpu.VMEM((HEADS, tq, LANES), F32)],
        compiler_params=pltpu.CompilerParams(dimension_semantics=("parallel", "parallel", "arbitrary"),
                                             vmem_limit_bytes=VMEM_LIMIT),
        name="attention",
    )(q, k, v)


def _wkv_kernel(rw_ref, m0_ref, tri_ref, y_ref, mout_ref, m_scr, *, nchunks):
    j = pl.program_id(1)

    @pl.when(j == 0)
    def _load():
        m_scr[...] = m0_ref[0]

    L = CHUNK
    r = rw_ref[:, 0:512]
    lw = rw_ref[:, 512:1024]
    k = rw_ref[:, 1024:1536]
    v = rw_ref[:, 1536:2048]
    a = rw_ref[:, 2048:2560]
    b = rw_ref[:, 2560:3072]
    c = _dot_hi(tri_ref[...], lw)
    gt = c[L - 1:L, :]
    en = jnp.exp(-c)
    eh = jnp.exp(gt - c)
    rt = r * jnp.exp(c)
    at = a * jnp.exp(c - lw)
    kt = k * en
    bt = b * en
    kh = k * eh
    bh = b * eh
    egt = jnp.exp(gt)

    lane = lax.broadcasted_iota(jnp.int32, (L, LANES), 1)
    even = lane < RW_HEAD
    ri = lax.broadcasted_iota(jnp.int32, (LANES, LANES), 0)
    ci = lax.broadcasted_iota(jnp.int32, (LANES, LANES), 1)
    tr = ri & (L - 1)
    tc = ci & (L - 1)
    strict = tr > tc
    incl = tr >= tc
    blk16 = (ri >> 4) == (ci >> 4)
    eye = ri == ci
    ident = jnp.where(eye, 1.0, 0.0).astype(F32)

    def stack(t):
        return jnp.concatenate([jnp.where(even, t, 0.0), jnp.where(even, 0.0, t)], axis=0)

    for p in range(HEAD_PAIRS):
        sl = slice(p * LANES, (p + 1) * LANES)
        a_s, r_s, b_s, k_s, bh_s, kh_s, v_s = [stack(t[:, sl]) for t in (at, rt, bt, kt, bh, kh, v)]
        m = m_scr[p]
        prod = _dot_nt(jnp.concatenate([a_s, r_s], axis=0), jnp.concatenate([b_s, k_s], axis=0), precision=HI)
        n_ab = jnp.where(strict, prod[0:LANES, 0:LANES], 0.0)
        a_ak = jnp.where(strict, prod[0:LANES, LANES:2 * LANES], 0.0)
        a_rb = jnp.where(incl, prod[LANES:2 * LANES, 0:LANES], 0.0)
        a_rk = jnp.where(incl, prod[LANES:2 * LANES, LANES:2 * LANES], 0.0)
        d1 = jnp.where(blk16, n_ab, 0.0)
        e1 = n_ab - d1
        p1 = ident + d1
        d2 = _dot_hi(d1, d1)
        p2 = p1 + _dot_hi(p1, d2)
        d4 = _dot_hi(d2, d2)
        p3 = p2 + _dot_hi(p2, d4)
        d8 = _dot_hi(d4, d4)
        t_d = p3 + _dot_hi(p3, d8)
        f1 = _dot_hi(t_d, e1)
        f2 = _dot_hi(f1, f1)
        if1 = ident + f1
        t_m = _dot_hi(if1 + _dot_hi(if1, f2), t_d)
        w = _dot_hi(jnp.concatenate([a_s, a_ak], axis=1), jnp.concatenate([m, v_s], axis=0))
        u = _dot_hi(t_m, w)
        y = _dot_hi(jnp.concatenate([r_s, a_rb, a_rk], axis=1), jnp.concatenate([m, u, v_s], axis=0))
        y_ref[:, sl] = y[0:L] + y[L:2 * L]
        dg = jnp.where(eye, egt[:, sl], 0.0)
        m_new = _dot_tn(jnp.concatenate([bh_s, kh_s, dg], axis=0), jnp.concatenate([u, v_s, m], axis=0), precision=HI)
        m_scr[p] = m_new

    @pl.when(j == nchunks - 1)
    def _store():
        mout_ref[0] = m_scr[...]


def _wkv_stage(rw, m0, tri, *, batch, seq):
    nchunks = seq // CHUNK
    assert seq % CHUNK == 0
    return pl.pallas_call(
        functools.partial(_wkv_kernel, nchunks=nchunks),
        grid=(batch, nchunks),
        in_specs=[pl.BlockSpec((CHUNK, 6 * RW_WIDTH), lambda b, j: (b * nchunks + j, 0)),
                  pl.BlockSpec((1, HEAD_PAIRS, LANES, LANES), lambda b, j: (b, 0, 0, 0)),
                  _const_spec((CHUNK, CHUNK))],
        out_specs=(pl.BlockSpec((CHUNK, RW_WIDTH), lambda b, j: (b * nchunks + j, 0)),
                   pl.BlockSpec((1, HEAD_PAIRS, LANES, LANES), lambda b, j: (b, 0, 0, 0))),
        out_shape=(jax.ShapeDtypeStruct((batch * seq, RW_WIDTH), F32),
                   jax.ShapeDtypeStruct((batch, HEAD_PAIRS, LANES, LANES), F32)),
        scratch_shapes=[pltpu.VMEM((HEAD_PAIRS, LANES, LANES), F32)],
        compiler_params=pltpu.CompilerParams(dimension_semantics=("parallel", "arbitrary"),
                                             vmem_limit_bytes=VMEM_LIMIT),
        name="wkv",
    )(rw, m0, tri)


def _out_kernel(x_ref, oa_ref, y_ref, bon_ref, ga_ref, gb_ref, ma_ref, mb_ref, v512_ref, seg_ref,
                woa_ref, wob_ref, wo_ref, out_ref):
    v512 = v512_ref[...]
    lnx_w, lnx_b = v512[5:6], v512[6:7]
    seg = seg_ref[...]
    oa = (oa_ref[...] * ga_ref[...]).astype(BF16)
    y = y_ref[...]
    d = y - _seg_sum(y, seg) * (1.0 / RW_HEAD)
    var = _seg_sum(d * d, seg) * (1.0 / RW_HEAD)
    yn = d * lax.rsqrt(var + GN_EPS) * lnx_w + lnx_b
    ob = ((yn + bon_ref[...]) * gb_ref[...]).astype(BF16)
    m = ma_ref[...] * _dot(oa, woa_ref[...]) + mb_ref[...] * _dot(ob, wob_ref[...])
    out_ref[...] = x_ref[...] + _dot(m.astype(BF16), wo_ref[...])


def _out_stage(x2d, oa, y, bon, ga, gb, ma, mb, lw):
    n = x2d.shape[0]
    tm = min(256, n)
    assert n % tm == 0
    row = lambda w: pl.BlockSpec((tm, w), lambda i: (i, 0))
    return pl.pallas_call(
        _out_kernel,
        grid=(n // tm,),
        in_specs=[row(D_MODEL), row(512), row(512), row(512), row(512), row(512), row(D_MODEL), row(D_MODEL),
                  _const_spec((8, RW_WIDTH)), _const_spec((RW_WIDTH, RW_WIDTH)),
                  _const_spec((512, D_MODEL)), _const_spec((512, D_MODEL)), _const_spec((D_MODEL, D_MODEL))],
        out_specs=row(D_MODEL),
        out_shape=jax.ShapeDtypeStruct((n, D_MODEL), F32),
        compiler_params=pltpu.CompilerParams(dimension_semantics=("parallel",), vmem_limit_bytes=VMEM_LIMIT),
        name="out_stage",
    )(x2d, oa, y, bon, ga, gb, ma, mb, lw["v512"], lw["seg"], lw["w_out_a"], lw["w_out_b"], lw["w_o"])


def _swap_halves(w):
    half = ROPE // 2
    return jnp.concatenate([w[..., half:], w[..., :half]], axis=-1)


def _prep_weights(norm_w, w_in, q_norm_w, kv_norm_w, w_uq, w_ukv, qn_nope, qn_rope, kn_nope, kn_rope,
                  mu_shift, w0, w2, a0, a2, k_k, k_a, r_k, lnx_w, lnx_b, w_out_a, w_out_b, w_o):
    depth = w_in.shape[0]
    z = lambda *s: jnp.zeros((depth,) + s, F32)
    kpe = w_in[..., 384:416]
    w_in_p = jnp.concatenate([w_in[..., :384], kpe, z(D_MODEL, 96), _swap_halves(kpe), z(D_MODEL, 96),
                              w_in[..., 416:]], axis=-1).astype(BF16)
    uq = w_uq.reshape(depth, Q_RANK, HEADS, QK)
    uq_p = jnp.concatenate([uq, z(Q_RANK, HEADS, 32)], axis=-1).reshape(depth, Q_RANK, HEADS * LANES)
    uq_s = jnp.concatenate([z(Q_RANK, HEADS, NOPE), _swap_halves(uq[..., NOPE:]), z(Q_RANK, HEADS, 32)],
                           axis=-1).reshape(depth, Q_RANK, HEADS * LANES)
    ukv = w_ukv.reshape(depth, KV_RANK, HEADS, 2 * NOPE)
    zk = z(KV_RANK, HEADS, NOPE)
    k_part = jnp.concatenate([ukv[..., :NOPE], zk], axis=-1)
    vv = ukv[..., NOPE:]
    odd = (jnp.arange(HEADS) % 2 == 1)[None, None, :, None]
    v_part = jnp.where(odd, jnp.concatenate([zk, vv], axis=-1), jnp.concatenate([vv, zk], axis=-1))
    ukv_p = jnp.concatenate([k_part.reshape(depth, KV_RANK, HEADS * LANES),
                             v_part.reshape(depth, KV_RANK, HEADS * LANES)], axis=-1).astype(BF16)
    zg = lambda n: jnp.zeros((depth, n), F32)
    g128 = jnp.stack([
        jnp.concatenate([qn_nope, qn_rope, zg(32)], axis=-1),
        jnp.concatenate([zg(NOPE), _swap_halves(qn_rope), zg(32)], axis=-1),
        jnp.concatenate([kn_rope, zg(96)], axis=-1),
        jnp.concatenate([_swap_halves(kn_rope), zg(96)], axis=-1),
        kv_norm_w,
        jnp.concatenate([kn_nope, zg(NOPE)], axis=-1),
        zg(LANES), zg(LANES)], axis=1)
    v512 = jnp.stack([w0, a0, k_k, k_a, r_k.reshape(depth, RW_WIDTH), lnx_w, lnx_b, zg(RW_WIDTH)], axis=1)
    seg_id = jnp.arange(RW_WIDTH) // RW_HEAD
    seg = (seg_id[:, None] == seg_id[None, :]).astype(BF16)
    onecol = jnp.stack([jnp.arange(LANES) == NOPE, jnp.arange(LANES) == 0]).astype(F32)
    return dict(
        norm_w=norm_w[:, None, :], w_in=w_in_p, q_norm_w=q_norm_w[:, None, :],
        w_uq=uq_p.astype(BF16), w_uq_sw=uq_s.astype(BF16), w_ukv=ukv_p, g128=g128, mu=mu_shift[:, None, :],
        v512=v512, w2=jnp.concatenate([w2, z(LORA, RW_WIDTH)], axis=1),
        a2=jnp.concatenate([z(LORA, RW_WIDTH), a2], axis=1),
        w_out_a=w_out_a.astype(BF16), w_out_b=w_out_b.astype(BF16), w_o=w_o.astype(BF16),
    ), dict(seg=seg, onecol=onecol)


def _rope_table(past, seq, batch):
    half = ROPE // 2
    inv = ROPE_THETA ** (-jnp.arange(half, dtype=F32) / half)
    pos = past + jnp.arange(seq, dtype=jnp.int32)
    ang = pos.astype(F32)[:, None] * inv[None, :]
    cos, sin = jnp.cos(ang), jnp.sin(ang)
    one, zero = jnp.ones((seq, NOPE), F32), jnp.zeros((seq, NOPE), F32)
    z32, z96 = jnp.zeros((seq, 32), F32), jnp.zeros((seq, 96), F32)
    tab = jnp.concatenate([one, cos, cos, z32, zero, -sin, sin, z32, cos, cos, z96, -sin, sin, z96], axis=-1)
    return jnp.tile(tab, (batch, 1))


def _state_to_pairs(s):
    b = s.shape[0]
    m = jnp.swapaxes(s.astype(F32), -1, -2).reshape(b, HEAD_PAIRS, 2, RW_HEAD, RW_HEAD)
    zero = jnp.zeros_like(m[:, :, 0])
    top = jnp.concatenate([m[:, :, 0], zero], axis=-1)
    bot = jnp.concatenate([zero, m[:, :, 1]], axis=-1)
    return jnp.concatenate([top, bot], axis=-2)


def _pairs_to_state(m):
    b = m.shape[0]
    e = m[:, :, :RW_HEAD, :RW_HEAD]
    o = m[:, :, RW_HEAD:, RW_HEAD:]
    return jnp.swapaxes(jnp.stack([e, o], axis=2).reshape(b, HEADS, RW_HEAD, RW_HEAD), -1, -2)


def _round_up(n, m):
    return (n + m - 1) // m * m


def _group_layer(x2d, tab, lw, ckv_past, kpe_past, m0, sh0, tri, *, batch, seq, past):
    n = batch * seq
    q, ckv, kpe, rw, bon, ga, gb, ma, mb, shift = _in_stage(x2d, tab, sh0, lw, seq=seq)
    ckv3 = ckv.reshape(batch, seq, KV_RANK)
    kpe3 = kpe.reshape(batch, seq, ROPE)
    kv_len = past + seq
    if past > 0:
        ckv_all = jnp.concatenate([ckv_past, ckv3], axis=1)
        kpe_all = jnp.concatenate([kpe_past, kpe3], axis=1)
    else:
        ckv_all, kpe_all = ckv3, kpe3
    if seq >= 512:
        tq, tk = 256, 512
        kv_pad = _round_up(kv_len, tk)
    else:
        tq = seq
        kv_pad = _round_up(kv_len, LANES)
        tk = kv_pad
    ckv_all = jnp.pad(ckv_all, ((0, 0), (0, kv_pad - kv_len), (0, 0)))
    kpe_all = jnp.pad(kpe_all, ((0, 0), (0, kv_pad - kv_len), (NOPE, LANES - QK)))
    kv_tile = max(t for t in (128, 256, 384, 512) if kv_pad % t == 0)
    k_att, v_att = _kv_stage(ckv_all.reshape(batch * kv_pad, KV_RANK), kpe_all.reshape(batch * kv_pad, LANES),
                             lw, tile=kv_tile)
    oa = _attn_stage(q, k_att, v_att, batch=batch, seq=seq, past=past, kv_len=kv_len, kv_pad=kv_pad, tq=tq, tk=tk)
    y, m1 = _wkv_stage(rw, m0, tri, batch=batch, seq=seq)
    out = _out_stage(x2d, oa, y, bon, ga, gb, ma, mb, lw)
    return out, ckv3, kpe3, _pairs_to_state(m1), shift.reshape(batch, 1, SHIFT_WIDTH)


def kernel(x_prompt, x_sample, cache_ckv, cache_kpe, state_wkv, state_shift, norm_w, w_in, q_norm_w, kv_norm_w,
           w_uq, w_ukv, qn_nope, qn_rope, kn_nope, kn_rope, mu_shift, w0, w2, a0, a2, k_k, k_a, r_k, lnx_w, lnx_b,
           w_out_a, w_out_b, w_o):
    depth = w_in.shape[0]
    bp, tp, _ = x_prompt.shape
    bs, ts, _ = x_sample.shape
    past = cache_ckv.shape[2]
    assert past % CHUNK == 0 and tp % CHUNK == 0 and ts % CHUNK == 0
    stacked, shared = _prep_weights(norm_w, w_in, q_norm_w, kv_norm_w, w_uq, w_ukv, qn_nope, qn_rope, kn_nope,
                                    kn_rope, mu_shift, w0, w2, a0, a2, k_k, k_a, r_k, lnx_w, lnx_b,
                                    w_out_a, w_out_b, w_o)
    tab_p = _rope_table(0, tp, bp)
    tab_s = _rope_table(past, ts, bs)
    tri = (jnp.arange(CHUNK)[:, None] >= jnp.arange(CHUNK)[None, :]).astype(F32)
    zero_m = jnp.zeros((bp, HEAD_PAIRS, LANES, LANES), F32)
    zero_sh = jnp.zeros((bp, 1, SHIFT_WIDTH), F32)

    yp = x_prompt.reshape(bp * tp, D_MODEL)
    ys = x_sample.reshape(bs * ts, D_MODEL)
    outs_p, outs_s = [], []
    for i in range(depth):
        lw = {name: val[i] for name, val in stacked.items()}
        lw.update(shared)
        yp, *rest_p = _group_layer(yp, tab_p, lw, None, None, zero_m, zero_sh, tri, batch=bp, seq=tp, past=0)
        outs_p.append(rest_p)
        ys, *rest_s = _group_layer(ys, tab_s, lw, cache_ckv[i], cache_kpe[i], _state_to_pairs(state_wkv[i]),
                                   state_shift[i], tri, batch=bs, seq=ts, past=past)
        outs_s.append(rest_s)
    stack = lambda outs, j: jnp.stack([o[j] for o in outs])
    return (yp.reshape(bp, tp, D_MODEL), ys.reshape(bs, ts, D_MODEL),
            stack(outs_p, 0), stack(outs_p, 1), stack(outs_p, 2), stack(outs_p, 3),
            stack(outs_s, 0), stack(outs_s, 1), stack(outs_s, 2), stack(outs_s, 3))
```

```python
import functools

import jax
import jax.numpy as jnp
from jax import lax
from jax.experimental import pallas as pl
from jax.experimental.pallas import tpu as pltpu

F32 = jnp.float32
BF16 = jnp.bfloat16
HI = lax.Precision.HIGHEST

D_MODEL = 1024
CHUNK = 64
CHUNK_SHIFT = 6
EPS = 1e-6
GN_EPS = 64e-5
HEADS = 8
NOPE = 64
ROPE = 32
QK = NOPE + ROPE
Q_RANK = 256
KV_RANK = 128
ROPE_THETA = 10000.0
RW_WIDTH = 512
RW_HEAD = 64
LORA = 64
SHIFT_WIDTH = 3 * RW_WIDTH + 2 * LORA
LANES = 128
HEAD_PAIRS = HEADS // 2
NEG = -1e30
LOG2_E = 1.4426950408889634

C_LAT = 0
C_GA = 640
C_ZS = C_GA + 512
C_GB = C_ZS + SHIFT_WIDTH
C_MA = C_GB + 512
C_MB = C_MA + D_MODEL
C_END = C_MB + D_MODEL

VMEM_LIMIT = 56 * 1024 * 1024


def _const_spec(shape):
    n = len(shape)
    return pl.BlockSpec(shape, lambda *_: (0,) * n)


def _dot(a, b):
    return jnp.dot(a, b, preferred_element_type=F32)


def _dot_hi(a, b):
    return jnp.dot(a, b, preferred_element_type=F32, precision=HI)


def _dot_nt(a, b, precision=None):
    return lax.dot_general(a, b, (((1,), (1,)), ((), ())), preferred_element_type=F32, precision=precision)


def _dot_tn(a, b, precision=None):
    return lax.dot_general(a, b, (((0,), (0,)), ((), ())), preferred_element_type=F32, precision=precision)


def _seg_sum(x, seg):
    hi = x.astype(BF16)
    lo = (x - hi.astype(F32)).astype(BF16)
    return _dot(hi, seg) + _dot(lo, seg)


def _in_kernel(x_ref, tab_ref, sh0_ref, normw_ref, win_ref, qnw_ref, wuq_ref, wuqs_ref, g128_ref, mu_ref,
               v512_ref, w2_ref, a2_ref, seg_ref,
               q_ref, ckv_ref, kpe_ref, rw_ref, bon_ref, ga_ref, gb_ref, ma_ref, mb_ref, shift_ref,
               carry_ref, *, tm, seq, scale):
    i = pl.program_id(0)
    x = x_ref[...]
    h = x * lax.rsqrt(jnp.mean(x * x, axis=-1, keepdims=True) + EPS) * normw_ref[...]
    hb = h.astype(BF16)
    tab = tab_ref[...]
    g128 = g128_ref[...]

    z1 = _dot(hb, win_ref[:, C_LAT:C_GA])
    cq = z1[:, :Q_RANK]
    cqn = cq * lax.rsqrt(jnp.mean(cq * cq, axis=-1, keepdims=True) + EPS) * qnw_ref[...]
    kvl = z1[:, Q_RANK:Q_RANK + KV_RANK]
    ckv_ref[...] = kvl * lax.rsqrt(jnp.mean(kvl * kvl, axis=-1, keepdims=True) + EPS) * g128[4:5]
    t1 = z1[:, 384:512]
    t2 = z1[:, 512:640]
    kinv = lax.rsqrt(jnp.sum(t1 * t1, axis=-1, keepdims=True) * (1.0 / ROPE) + EPS)
    kr = (t1 * g128[2:3] * tab[:, 256:384] + t2 * g128[3:4] * tab[:, 384:512]) * kinv
    kpe_ref[...] = kr[:, :ROPE]

    cqb = cqn.astype(BF16)
    qr = _dot(cqb, wuq_ref[...])
    qs = _dot(cqb, wuqs_ref[...])
    lane = lax.broadcasted_iota(jnp.int32, (1, LANES), 1)
    is_nope = lane < NOPE
    cos_f = tab[:, 0:128]
    sin_f = tab[:, 128:256]
    for hd in range(HEADS):
        t = qr[:, hd * LANES:(hd + 1) * LANES]
        sq = t * t
        ms_n = jnp.sum(jnp.where(is_nope, sq, 0.0), axis=-1, keepdims=True) * (1.0 / NOPE)
        ms_r = jnp.sum(jnp.where(is_nope, 0.0, sq), axis=-1, keepdims=True) * (1.0 / ROPE)
        inv = jnp.where(is_nope, lax.rsqrt(ms_n + EPS), lax.rsqrt(ms_r + EPS))
        qh = (t * g128[0:1] * cos_f + qs[:, hd * LANES:(hd + 1) * LANES] * g128[1:2] * sin_f) * (inv * scale)
        q_ref[hd] = qh.astype(BF16)

    ga_ref[...] = jax.nn.silu(_dot(hb, win_ref[:, C_GA:C_ZS]))
    gb_ref[...] = jax.nn.silu(_dot(hb, win_ref[:, C_GB:C_MA]))
    ma_ref[...] = jax.nn.sigmoid(_dot(hb, win_ref[:, C_MA:C_MB]))
    mb_ref[...] = jax.nn.sigmoid(_dot(hb, win_ref[:, C_MB:C_END]))

    zs = _dot(hb, win_ref[:, C_ZS:C_GB])
    rows = lax.broadcasted_iota(jnp.int32, (tm, 1), 0)
    prev = pltpu.roll(zs, 1, axis=0)
    if seq >= tm:
        tiles_per_seq = seq // tm

        @pl.when(i == 0)
        def _init_carry():
            carry_ref[...] = jnp.zeros(carry_ref.shape, F32)

        first = jnp.where(i % tiles_per_seq == 0, sh0_ref[0], carry_ref[...])
        prev = jnp.where(rows == 0, first, prev)
        carry_ref[...] = zs[tm - 1:tm, :]
        shift_ref[0] = zs[tm - 1:tm, :]
    else:
        for s in range(tm // seq):
            prev = jnp.where(rows == s * seq, sh0_ref[0, s:s + 1, :], prev)
            shift_ref[0, s:s + 1, :] = zs[(s + 1) * seq - 1:(s + 1) * seq, :]
    zm = zs + (prev - zs) * mu_ref[...]
    r = zm[:, 0:512]
    k = zm[:, 512:1024]
    v = zm[:, 1024:1536]
    t13 = zm[:, 1536:1664]
    v512 = v512_ref[...]
    w0, a0, k_k, k_a, r_k = v512[0:1], v512[1:2], v512[2:3], v512[3:4], v512[4:5]
    seg = seg_ref[...]
    log_w = -jax.nn.softplus(-(w0 + _dot_hi(jnp.tanh(t13), w2_ref[...]))) - 0.5
    lw = -jnp.exp(log_w)
    aic = jax.nn.sigmoid(a0 + _dot_hi(t13, a2_ref[...]))
    kkr = k * k_k
    kk = kkr / jnp.maximum(jnp.sqrt(_seg_sum(kkr * kkr, seg)), 1e-12)
    k2 = k * (1.0 + (aic - 1.0) * k_a)
    bon_ref[...] = _seg_sum(r * k2 * r_k, seg) * v
    rw_ref[:, 0:512] = r
    rw_ref[:, 512:1024] = lw
    rw_ref[:, 1024:1536] = k2
    rw_ref[:, 1536:2048] = v
    rw_ref[:, 2048:2560] = -kk
    rw_ref[:, 2560:3072] = kk * aic


def _in_stage(x2d, tab, sh0, lw, *, seq):
    n = x2d.shape[0]
    tm = 256 if seq >= 256 else min(256, n)
    assert n % tm == 0 and (seq % tm == 0 or tm % seq == 0)
    nseq = max(1, tm // seq)
    batch = n // seq
    sh0 = sh0.reshape(batch // nseq, nseq, SHIFT_WIDTH)
    if seq >= tm:
        sh_map = lambda i: (i // (seq // tm), 0, 0)
    else:
        sh_map = lambda i: (i, 0, 0)
    row = lambda w: pl.BlockSpec((tm, w), lambda i: (i, 0))
    out_shape = (
        jax.ShapeDtypeStruct((HEADS, n, LANES), BF16),
        jax.ShapeDtypeStruct((n, KV_RANK), F32),
        jax.ShapeDtypeStruct((n, ROPE), F32),
        jax.ShapeDtypeStruct((n, 6 * RW_WIDTH), F32),
        jax.ShapeDtypeStruct((n, RW_WIDTH), F32),
        jax.ShapeDtypeStruct((n, 512), F32),
        jax.ShapeDtypeStruct((n, 512), F32),
        jax.ShapeDtypeStruct((n, D_MODEL), F32),
        jax.ShapeDtypeStruct((n, D_MODEL), F32),
        jax.ShapeDtypeStruct((batch // nseq, nseq, SHIFT_WIDTH), F32),
    )
    out_specs = (
        pl.BlockSpec((HEADS, tm, LANES), lambda i: (0, i, 0)),
        row(KV_RANK), row(ROPE), row(6 * RW_WIDTH), row(RW_WIDTH), row(512), row(512), row(D_MODEL), row(D_MODEL),
        pl.BlockSpec((1, nseq, SHIFT_WIDTH), sh_map),
    )
    in_specs = [
        row(D_MODEL), row(512), pl.BlockSpec((1, nseq, SHIFT_WIDTH), sh_map),
        _const_spec((1, D_MODEL)),
        pl.BlockSpec((D_MODEL, C_END), lambda i: (0, 0), pipeline_mode=pl.Buffered(1)),
        _const_spec((1, Q_RANK)), _const_spec((Q_RANK, HEADS * LANES)), _const_spec((Q_RANK, HEADS * LANES)),
        _const_spec((8, LANES)), _const_spec((1, SHIFT_WIDTH)), _const_spec((8, RW_WIDTH)),
        _const_spec((LANES, RW_WIDTH)), _const_spec((LANES, RW_WIDTH)), _const_spec((RW_WIDTH, RW_WIDTH)),
    ]
    return pl.pallas_call(
        functools.partial(_in_kernel, tm=tm, seq=seq, scale=QK ** -0.5 * LOG2_E),
        grid=(n // tm,),
        in_specs=in_specs,
        out_specs=out_specs,
        out_shape=out_shape,
        scratch_shapes=[pltpu.VMEM((1, SHIFT_WIDTH), F32)],
        compiler_params=pltpu.CompilerParams(dimension_semantics=("arbitrary",), vmem_limit_bytes=VMEM_LIMIT),
        name="in_stage",
    )(x2d, tab, sh0, lw["norm_w"], lw["w_in"], lw["q_norm_w"], lw["w_uq"], lw["w_uq_sw"], lw["g128"], lw["mu"],
      lw["v512"], lw["w2"], lw["a2"], lw["seg"])


def _kv_kernel(ckv_ref, kpe_ref, wukv_ref, g128_ref, one_ref, k_ref, v_ref):
    kv = _dot(ckv_ref[...].astype(BF16), wukv_ref[...])
    kpe = kpe_ref[...]
    gkn = g128_ref[5:6]
    for hd in range(HEADS):
        kr = kv[:, hd * LANES:(hd + 1) * LANES]
        inv = lax.rsqrt(jnp.sum(kr * kr, axis=-1, keepdims=True) * (1.0 / NOPE) + EPS)
        k_ref[hd] = (kr * inv * gkn + kpe).astype(BF16)
        vr = kv[:, (HEADS + hd) * LANES:(HEADS + hd + 1) * LANES]
        v_ref[hd] = (vr + one_ref[hd % 2:hd % 2 + 1]).astype(BF16)


def _kv_stage(ckv2d, kpe2d, lw, *, tile):
    n = ckv2d.shape[0]
    assert n % tile == 0
    row = pl.BlockSpec((tile, LANES), lambda i: (i, 0))
    hspec = pl.BlockSpec((HEADS, tile, LANES), lambda i: (0, i, 0))
    return pl.pallas_call(
        _kv_kernel,
        grid=(n // tile,),
        in_specs=[row, row, _const_spec((KV_RANK, 2 * HEADS * LANES)), _const_spec((8, LANES)),
                  _const_spec((2, LANES))],
        out_specs=(hspec, hspec),
        out_shape=(jax.ShapeDtypeStruct((HEADS, n, LANES), BF16),) * 2,
        compiler_params=pltpu.CompilerParams(dimension_semantics=("parallel",), vmem_limit_bytes=VMEM_LIMIT),
        name="kv_stage",
    )(ckv2d, kpe2d, lw["w_ukv"], lw["g128"], lw["onecol"])


def _attn_kernel(q_ref, k_ref, v_ref, o_ref, m_scr, acc_scr, *, tq, tk, past, kv_len, nk):
    qi = pl.program_id(1)
    kj = pl.program_id(2)

    @pl.when(kj == 0)
    def _init():
        m_scr[...] = jnp.full(m_scr.shape, NEG, F32)
        acc_scr[...] = jnp.zeros(acc_scr.shape, F32)

    q_first = past + qi * tq
    q_last = q_first + tq - 1
    k_first = kj * tk
    k_last = k_first + tk - 1
    needed = (k_first >> CHUNK_SHIFT) <= (q_last >> CHUNK_SHIFT)
    unmasked = jnp.logical_and((k_last >> CHUNK_SHIFT) <= (q_first >> CHUNK_SHIFT), k_last < kv_len)

    def compute(masked):
        if masked:
            rows = q_first + lax.broadcasted_iota(jnp.int32, (tq, tk), 0)
            cols = k_first + lax.broadcasted_iota(jnp.int32, (tq, tk), 1)
            ok = jnp.logical_and((cols >> CHUNK_SHIFT) <= (rows >> CHUNK_SHIFT), cols < kv_len)
        s_next = _dot_nt(q_ref[0], k_ref[0])
        for hd in range(HEADS):
            s = s_next
            if hd + 1 < HEADS:
                s_next = _dot_nt(q_ref[hd + 1], k_ref[hd + 1])
            if masked:
                s = jnp.where(ok, s, NEG)
            m_prev = m_scr[hd]
            m_new = jnp.maximum(m_prev, jnp.max(s, axis=-1, keepdims=True))
            alpha = jnp.exp2(m_prev - m_new)
            p = jnp.exp2(s - jnp.concatenate([m_new] * (tk // LANES), axis=1))
            acc_scr[hd] = acc_scr[hd] * alpha + _dot(p.astype(BF16), v_ref[hd])
            m_scr[hd] = m_new

    @pl.when(jnp.logical_and(needed, unmasked))
    def _full():
        compute(False)

    @pl.when(jnp.logical_and(needed, jnp.logical_not(unmasked)))
    def _diag():
        compute(True)

    @pl.when(kj == nk - 1)
    def _finish():
        lane = lax.broadcasted_iota(jnp.int32, (1, LANES), 1)
        for p in range(HEAD_PAIRS):
            ae = acc_scr[2 * p]
            ao = acc_scr[2 * p + 1]
            o_ref[:, p * LANES:(p + 1) * LANES] = jnp.where(lane < 64, ae / ae[:, 64:65], ao / ao[:, 0:1])


def _attn_stage(q, k, v, *, batch, seq, past, kv_len, kv_pad, tq, tk):
    nq = seq // tq
    nk = kv_pad // tk
    assert seq % tq == 0 and kv_pad % tk == 0

    def kv_map(b, qi, kj):
        last = jnp.minimum((past + qi * tq + tq - 1) // tk, nk - 1)
        return (0, b * nk + jnp.minimum(kj, last), 0)

    return pl.pallas_call(
        functools.partial(_attn_kernel, tq=tq, tk=tk, past=past, kv_len=kv_len, nk=nk),
        grid=(batch, nq, nk),
        in_specs=[pl.BlockSpec((HEADS, tq, LANES), lambda b, qi, kj: (0, b * nq + qi, 0)),
                  pl.BlockSpec((HEADS, tk, LANES), kv_map),
                  pl.BlockSpec((HEADS, tk, LANES), kv_map)],
        out_specs=pl.BlockSpec((tq, HEADS * NOPE), lambda b, qi, kj: (b * nq + qi, 0)),
        out_shape=jax.ShapeDtypeStruct((batch * seq, HEADS * NOPE), F32),
        scratch_shapes=[pltpu.VMEM((HEADS, tq, LANES), F32), pltpu.VMEM((HEADS, tq, LANES), F32)],
        compiler_params=pltpu.CompilerParams(dimension_semantics=("parallel", "parallel", "arbitrary"),
                                             vmem_limit_bytes=VMEM_LIMIT),
        name="attention",
    )(q, k, v)


def _wkv_kernel(rw_ref, m0_ref, tri_ref, y_ref, mout_ref, m_scr, *, nchunks):
    j = pl.program_id(1)

    @pl.when(j == 0)
    def _load():
        m_scr[...] = m0_ref[0]

    L = CHUNK
    r = rw_ref[:, 0:512]
    lw = rw_ref[:, 512:1024]
    k = rw_ref[:, 1024:1536]
    v = rw_ref[:, 1536:2048]
    a = rw_ref[:, 2048:2560]
    b = rw_ref[:, 2560:3072]
    c = _dot_hi(tri_ref[...], lw)
    gt = c[L - 1:L, :]
    en = jnp.exp(-c)
    eh = jnp.exp(gt - c)
    rt = r * jnp.exp(c)
    at = a * jnp.exp(c - lw)
    kt = k * en
    bt = b * en
    kh = k * eh
    bh = b * eh
    egt = jnp.exp(gt)

    lane = lax.broadcasted_iota(jnp.int32, (L, LANES), 1)
    even = lane < RW_HEAD
    ri = lax.broadcasted_iota(jnp.int32, (LANES, LANES), 0)
    ci = lax.broadcasted_iota(jnp.int32, (LANES, LANES), 1)
    tr = ri & (L - 1)
    tc = ci & (L - 1)
    strict = tr > tc
    incl = tr >= tc
    blk16 = (ri >> 4) == (ci >> 4)
    eye = ri == ci
    ident = jnp.where(eye, 1.0, 0.0).astype(F32)

    def stack(t):
        return jnp.concatenate([jnp.where(even, t, 0.0), jnp.where(even, 0.0, t)], axis=0)

    pairs = range(HEAD_PAIRS)
    each = lambda fn, *cols: [fn(*args) for args in zip(*cols)]
    cat0 = lambda *xs: jnp.concatenate(xs, axis=0)
    cat1 = lambda *xs: jnp.concatenate(xs, axis=1)
    sls = [slice(p * LANES, (p + 1) * LANES) for p in pairs]
    a_s, r_s, b_s, k_s, bh_s, kh_s, v_s = [[stack(t[:, sl]) for sl in sls] for t in (at, rt, bt, kt, bh, kh, v)]
    m = [m_scr[p] for p in pairs]
    prod = each(lambda a_, r_, b_, k_: _dot_nt(cat0(a_, r_), cat0(b_, k_), precision=HI), a_s, r_s, b_s, k_s)
    n_ab = [jnp.where(strict, x[0:LANES, 0:LANES], 0.0) for x in prod]
    a_ak = [jnp.where(strict, x[0:LANES, LANES:2 * LANES], 0.0) for x in prod]
    a_rb = [jnp.where(incl, x[LANES:2 * LANES, 0:LANES], 0.0) for x in prod]
    a_rk = [jnp.where(incl, x[LANES:2 * LANES, LANES:2 * LANES], 0.0) for x in prod]
    d1 = [jnp.where(blk16, x, 0.0) for x in n_ab]
    e1 = each(lambda x, d: x - d, n_ab, d1)
    p1 = [ident + d for d in d1]
    d2 = each(_dot_hi, d1, d1)
    p2 = each(lambda p_, d: p_ + _dot_hi(p_, d), p1, d2)
    d4 = each(_dot_hi, d2, d2)
    p3 = each(lambda p_, d: p_ + _dot_hi(p_, d), p2, d4)
    d8 = each(_dot_hi, d4, d4)
    t_d = each(lambda p_, d: p_ + _dot_hi(p_, d), p3, d8)
    f1 = each(_dot_hi, t_d, e1)
    f2 = each(_dot_hi, f1, f1)
    if1 = [ident + f for f in f1]
    g1 = each(lambda i_, f: i_ + _dot_hi(i_, f), if1, f2)
    t_m = each(_dot_hi, g1, t_d)
    w = each(lambda a_, ak, m_, v_: _dot_hi(cat1(a_, ak), cat0(m_, v_)), a_s, a_ak, m, v_s)
    u = each(_dot_hi, t_m, w)
    y = each(lambda r_, rb, rk, m_, u_, v_: _dot_hi(cat1(r_, rb, rk), cat0(m_, u_, v_)), r_s, a_rb, a_rk, m, u, v_s)
    m_new = each(lambda b_, k_, sl, u_, v_, m_: _dot_tn(cat0(b_, k_, jnp.where(eye, egt[:, sl], 0.0)),
                                                        cat0(u_, v_, m_), precision=HI),
                 bh_s, kh_s, sls, u, v_s, m)
    for p in pairs:
        y_ref[:, sls[p]] = y[p][0:L] + y[p][L:2 * L]
        m_scr[p] = m_new[p]

    @pl.when(j == nchunks - 1)
    def _store():
        mout_ref[0] = m_scr[...]


def _wkv_stage(rw, m0, tri, *, batch, seq):
    nchunks = seq // CHUNK
    assert seq % CHUNK == 0
    return pl.pallas_call(
        functools.partial(_wkv_kernel, nchunks=nchunks),
        grid=(batch, nchunks),
        in_specs=[pl.BlockSpec((CHUNK, 6 * RW_WIDTH), lambda b, j: (b * nchunks + j, 0)),
                  pl.BlockSpec((1, HEAD_PAIRS, LANES, LANES), lambda b, j: (b, 0, 0, 0)),
                  _const_spec((CHUNK, CHUNK))],
        out_specs=(pl.BlockSpec((CHUNK, RW_WIDTH), lambda b, j: (b * nchunks + j, 0)),
                   pl.BlockSpec((1, HEAD_PAIRS, LANES, LANES), lambda b, j: (b, 0, 0, 0))),
        out_shape=(jax.ShapeDtypeStruct((batch * seq, RW_WIDTH), F32),
                   jax.ShapeDtypeStruct((batch, HEAD_PAIRS, LANES, LANES), F32)),
        scratch_shapes=[pltpu.VMEM((HEAD_PAIRS, LANES, LANES), F32)],
        compiler_params=pltpu.CompilerParams(dimension_semantics=("parallel", "arbitrary"),
                                             vmem_limit_bytes=VMEM_LIMIT),
        name="wkv",
    )(rw, m0, tri)


def _out_kernel(x_ref, oa_ref, y_ref, bon_ref, ga_ref, gb_ref, ma_ref, mb_ref, v512_ref, seg_ref,
                woa_ref, wob_ref, wo_ref, out_ref):
    v512 = v512_ref[...]
    lnx_w, lnx_b = v512[5:6], v512[6:7]
    seg = seg_ref[...]
    oa = (oa_ref[...] * ga_ref[...]).astype(BF16)
    y = y_ref[...]
    d = y - _seg_sum(y, seg) * (1.0 / RW_HEAD)
    var = _seg_sum(d * d, seg) * (1.0 / RW_HEAD)
    yn = d * lax.rsqrt(var + GN_EPS) * lnx_w + lnx_b
    ob = ((yn + bon_ref[...]) * gb_ref[...]).astype(BF16)
    m = ma_ref[...] * _dot(oa, woa_ref[...]) + mb_ref[...] * _dot(ob, wob_ref[...])
    out_ref[...] = x_ref[...] + _dot(m.astype(BF16), wo_ref[...])


def _out_stage(x2d, oa, y, bon, ga, gb, ma, mb, lw):
    n = x2d.shape[0]
    tm = min(256, n)
    assert n % tm == 0
    row = lambda w: pl.BlockSpec((tm, w), lambda i: (i, 0))
    return pl.pallas_call(
        _out_kernel,
        grid=(n // tm,),
        in_specs=[row(D_MODEL), row(512), row(512), row(512), row(512), row(512), row(D_MODEL), row(D_MODEL),
                  _const_spec((8, RW_WIDTH)), _const_spec((RW_WIDTH, RW_WIDTH)),
                  _const_spec((512, D_MODEL)), _const_spec((512, D_MODEL)), _const_spec((D_MODEL, D_MODEL))],
        out_specs=row(D_MODEL),
        out_shape=jax.ShapeDtypeStruct((n, D_MODEL), F32),
        compiler_params=pltpu.CompilerParams(dimension_semantics=("parallel",), vmem_limit_bytes=VMEM_LIMIT),
        name="out_stage",
    )(x2d, oa, y, bon, ga, gb, ma, mb, lw["v512"], lw["seg"], lw["w_out_a"], lw["w_out_b"], lw["w_o"])


def _swap_halves(w):
    half = ROPE // 2
    return jnp.concatenate([w[..., half:], w[..., :half]], axis=-1)


def _prep_weights(norm_w, w_in, q_norm_w, kv_norm_w, w_uq, w_ukv, qn_nope, qn_rope, kn_nope, kn_rope,
                  mu_shift, w0, w2, a0, a2, k_k, k_a, r_k, lnx_w, lnx_b, w_out_a, w_out_b, w_o):
    depth = w_in.shape[0]
    z = lambda *s: jnp.zeros((depth,) + s, F32)
    kpe = w_in[..., 384:416]
    w_in_p = jnp.concatenate([w_in[..., :384], kpe, z(D_MODEL, 96), _swap_halves(kpe), z(D_MODEL, 96),
                              w_in[..., 416:]], axis=-1).astype(BF16)
    uq = w_uq.reshape(depth, Q_RANK, HEADS, QK)
    uq_p = jnp.concatenate([uq, z(Q_RANK, HEADS, 32)], axis=-1).reshape(depth, Q_RANK, HEADS * LANES)
    uq_s = jnp.concatenate([z(Q_RANK, HEADS, NOPE), _swap_halves(uq[..., NOPE:]), z(Q_RANK, HEADS, 32)],
                           axis=-1).reshape(depth, Q_RANK, HEADS * LANES)
    ukv = w_ukv.reshape(depth, KV_RANK, HEADS, 2 * NOPE)
    zk = z(KV_RANK, HEADS, NOPE)
    k_part = jnp.concatenate([ukv[..., :NOPE], zk], axis=-1)
    vv = ukv[..., NOPE:]
    odd = (jnp.arange(HEADS) % 2 == 1)[None, None, :, None]
    v_part = jnp.where(odd, jnp.concatenate([zk, vv], axis=-1), jnp.concatenate([vv, zk], axis=-1))
    ukv_p = jnp.concatenate([k_part.reshape(depth, KV_RANK, HEADS * LANES),
                             v_part.reshape(depth, KV_RANK, HEADS * LANES)], axis=-1).astype(BF16)
    zg = lambda n: jnp.zeros((depth, n), F32)
    g128 = jnp.stack([
        jnp.concatenate([qn_nope, qn_rope, zg(32)], axis=-1),
        jnp.concatenate([zg(NOPE), _swap_halves(qn_rope), zg(32)], axis=-1),
        jnp.concatenate([kn_rope, zg(96)], axis=-1),
        jnp.concatenate([_swap_halves(kn_rope), zg(96)], axis=-1),
        kv_norm_w,
        jnp.concatenate([kn_nope, zg(NOPE)], axis=-1),
        zg(LANES), zg(LANES)], axis=1)
    v512 = jnp.stack([w0, a0, k_k, k_a, r_k.reshape(depth, RW_WIDTH), lnx_w, lnx_b, zg(RW_WIDTH)], axis=1)
    seg_id = jnp.arange(RW_WIDTH) // RW_HEAD
    seg = (seg_id[:, None] == seg_id[None, :]).astype(BF16)
    onecol = jnp.stack([jnp.arange(LANES) == NOPE, jnp.arange(LANES) == 0]).astype(F32)
    return dict(
        norm_w=norm_w[:, None, :], w_in=w_in_p, q_norm_w=q_norm_w[:, None, :],
        w_uq=uq_p.astype(BF16), w_uq_sw=uq_s.astype(BF16), w_ukv=ukv_p, g128=g128, mu=mu_shift[:, None, :],
        v512=v512, w2=jnp.concatenate([w2, z(LORA, RW_WIDTH)], axis=1),
        a2=jnp.concatenate([z(LORA, RW_WIDTH), a2], axis=1),
        w_out_a=w_out_a.astype(BF16), w_out_b=w_out_b.astype(BF16), w_o=w_o.astype(BF16),
    ), dict(seg=seg, onecol=onecol)


def _rope_table(past, seq, batch):
    half = ROPE // 2
    inv = ROPE_THETA ** (-jnp.arange(half, dtype=F32) / half)
    pos = past + jnp.arange(seq, dtype=jnp.int32)
    ang = pos.astype(F32)[:, None] * inv[None, :]
    cos, sin = jnp.cos(ang), jnp.sin(ang)
    one, zero = jnp.ones((seq, NOPE), F32), jnp.zeros((seq, NOPE), F32)
    z32, z96 = jnp.zeros((seq, 32), F32), jnp.zeros((seq, 96), F32)
    tab = jnp.concatenate([one, cos, cos, z32, zero, -sin, sin, z32, cos, cos, z96, -sin, sin, z96], axis=-1)
    return jnp.tile(tab, (batch, 1))


def _state_to_pairs(s):
    b = s.shape[0]
    m = jnp.swapaxes(s.astype(F32), -1, -2).reshape(b, HEAD_PAIRS, 2, RW_HEAD, RW_HEAD)
    zero = jnp.zeros_like(m[:, :, 0])
    top = jnp.concatenate([m[:, :, 0], zero], axis=-1)
    bot = jnp.concatenate([zero, m[:, :, 1]], axis=-1)
    return jnp.concatenate([top, bot], axis=-2)


def _pairs_to_state(m):
    b = m.shape[0]
    e = m[:, :, :RW_HEAD, :RW_HEAD]
    o = m[:, :, RW_HEAD:, RW_HEAD:]
    return jnp.swapaxes(jnp.stack([e, o], axis=2).reshape(b, HEADS, RW_HEAD, RW_HEAD), -1, -2)


def _round_up(n, m):
    return (n + m - 1) // m * m


def _group_layer(x2d, tab, lw, ckv_past, kpe_past, m0, sh0, tri, *, batch, seq, past):
    n = batch * seq
    q, ckv, kpe, rw, bon, ga, gb, ma, mb, shift = _in_stage(x2d, tab, sh0, lw, seq=seq)
    ckv3 = ckv.reshape(batch, seq, KV_RANK)
    kpe3 = kpe.reshape(batch, seq, ROPE)
    kv_len = past + seq
    if past > 0:
        ckv_all = jnp.concatenate([ckv_past, ckv3], axis=1)
        kpe_all = jnp.concatenate([kpe_past, kpe3], axis=1)
    else:
        ckv_all, kpe_all = ckv3, kpe3
    if seq >= 512:
        tq, tk = 256, 512
        kv_pad = _round_up(kv_len, tk)
    else:
        tq = seq
        kv_pad = _round_up(kv_len, LANES)
        tk = kv_pad
    ckv_all = jnp.pad(ckv_all, ((0, 0), (0, kv_pad - kv_len), (0, 0)))
    kpe_all = jnp.pad(kpe_all, ((0, 0), (0, kv_pad - kv_len), (NOPE, LANES - QK)))
    kv_tile = max(t for t in (128, 256, 384, 512) if kv_pad % t == 0)
    k_att, v_att = _kv_stage(ckv_all.reshape(batch * kv_pad, KV_RANK), kpe_all.reshape(batch * kv_pad, LANES),
                             lw, tile=kv_tile)
    oa = _attn_stage(q, k_att, v_att, batch=batch, seq=seq, past=past, kv_len=kv_len, kv_pad=kv_pad, tq=tq, tk=tk)
    y, m1 = _wkv_stage(rw, m0, tri, batch=batch, seq=seq)
    out = _out_stage(x2d, oa, y, bon, ga, gb, ma, mb, lw)
    return out, ckv3, kpe3, _pairs_to_state(m1), shift.reshape(batch, 1, SHIFT_WIDTH)


def kernel(x_prompt, x_sample, cache_ckv, cache_kpe, state_wkv, state_shift, norm_w, w_in, q_norm_w, kv_norm_w,
           w_uq, w_ukv, qn_nope, qn_rope, kn_nope, kn_rope, mu_shift, w0, w2, a0, a2, k_k, k_a, r_k, lnx_w, lnx_b,
           w_out_a, w_out_b, w_o):
    depth = w_in.shape[0]
    bp, tp, _ = x_prompt.shape
    bs, ts, _ = x_sample.shape
    past = cache_ckv.shape[2]
    assert past % CHUNK == 0 and tp % CHUNK == 0 and ts % CHUNK == 0
    stacked, shared = _prep_weights(norm_w, w_in, q_norm_w, kv_norm_w, w_uq, w_ukv, qn_nope, qn_rope, kn_nope,
                                    kn_rope, mu_shift, w0, w2, a0, a2, k_k, k_a, r_k, lnx_w, lnx_b,
                                    w_out_a, w_out_b, w_o)
    tab_p = _rope_table(0, tp, bp)
    tab_s = _rope_table(past, ts, bs)
    tri = (jnp.arange(CHUNK)[:, None] >= jnp.arange(CHUNK)[None, :]).astype(F32)
    zero_m = jnp.zeros((bp, HEAD_PAIRS, LANES, LANES), F32)
    zero_sh = jnp.zeros((bp, 1, SHIFT_WIDTH), F32)

    yp = x_prompt.reshape(bp * tp, D_MODEL)
    ys = x_sample.reshape(bs * ts, D_MODEL)
    outs_p, outs_s = [], []
    for i in range(depth):
        lw = {name: val[i] for name, val in stacked.items()}
        lw.update(shared)
        yp, *rest_p = _group_layer(yp, tab_p, lw, None, None, zero_m, zero_sh, tri, batch=bp, seq=tp, past=0)
        outs_p.append(rest_p)
        ys, *rest_s = _group_layer(ys, tab_s, lw, cache_ckv[i], cache_kpe[i], _state_to_pairs(state_wkv[i]),
                                   state_shift[i], tri, batch=bs, seq=ts, past=past)
        outs_s.append(rest_s)
    stack = lambda outs, j: jnp.stack([o[j] for o in outs])
    return (yp.reshape(bp, tp, D_MODEL), ys.reshape(bs, ts, D_MODEL),
            stack(outs_p, 0), stack(outs_p, 1), stack(outs_p, 2), stack(outs_p, 3),
            stack(outs_s, 0), stack(outs_s, 1), stack(outs_s, 2), stack(outs_s, 3))
```

```python
import functools

import jax
import jax.numpy as jnp
from jax import lax
from jax.experimental import pallas as pl
from jax.experimental.pallas import tpu as pltpu

F32 = jnp.float32
BF16 = jnp.bfloat16
HI = lax.Precision.HIGHEST

D_MODEL = 1024
CHUNK = 64
CHUNK_SHIFT = 6
EPS = 1e-6
GN_EPS = 64e-5
HEADS = 8
NOPE = 64
ROPE = 32
QK = NOPE + ROPE
Q_RANK = 256
KV_RANK = 128
ROPE_THETA = 10000.0
RW_WIDTH = 512
RW_HEAD = 64
LORA = 64
SHIFT_WIDTH = 3 * RW_WIDTH + 2 * LORA
LANES = 128
HEAD_PAIRS = HEADS // 2
NEG = -1e30
P_PROD, P_INV, P_W, P_U, P_Y, P_M = 1, 1, 1, 1, 1, 1
LOG2_E = 1.4426950408889634

C_LAT = 0
C_GA = 640
C_ZS = C_GA + 512
C_GB = C_ZS + SHIFT_WIDTH
C_MA = C_GB + 512
C_MB = C_MA + D_MODEL
C_END = C_MB + D_MODEL

VMEM_LIMIT = 56 * 1024 * 1024


def _const_spec(shape):
    n = len(shape)
    return pl.BlockSpec(shape, lambda *_: (0,) * n)


_NN = (((1,), (0,)), ((), ()))
_NT = (((1,), (1,)), ((), ()))
_TN = (((0,), (0,)), ((), ()))


def _dot(a, b):
    return jnp.dot(a, b, preferred_element_type=F32)


def _dot_nt(a, b):
    return lax.dot_general(a, b, _NT, preferred_element_type=F32)


def _split(x):
    hi = x.astype(BF16)
    return hi, (x - hi.astype(F32)).astype(BF16)


def _mx(a, b, passes, dims=_NN):
    if passes == 6:
        return lax.dot_general(a, b, dims, precision=HI, preferred_element_type=F32)
    dg = lambda x, y: lax.dot_general(x, y, dims, preferred_element_type=F32)
    if passes == 1:
        return dg(a.astype(BF16), b.astype(BF16))
    ah, al = _split(a)
    bh, bl = _split(b)
    return dg(ah, bh) + (dg(ah, bl) + dg(al, bh))


def _seg_sum(x, seg):
    hi = x.astype(BF16)
    lo = (x - hi.astype(F32)).astype(BF16)
    return _dot(hi, seg) + _dot(lo, seg)


def _in_kernel(x_ref, tab_ref, sh0_ref, normw_ref, win_ref, qnw_ref, wuq_ref, wuqs_ref, g128_ref, mu_ref,
               v512_ref, w2_ref, a2_ref, seg_ref,
               q_ref, ckv_ref, kpe_ref, rw_ref, bon_ref, ga_ref, gb_ref, ma_ref, mb_ref, shift_ref,
               carry_ref, *, tm, seq, scale):
    i = pl.program_id(0)
    x = x_ref[...]
    h = x * lax.rsqrt(jnp.mean(x * x, axis=-1, keepdims=True) + EPS) * normw_ref[...]
    hb = h.astype(BF16)
    tab = tab_ref[...]
    g128 = g128_ref[...]

    z1 = _dot(hb, win_ref[:, C_LAT:C_GA])
    cq = z1[:, :Q_RANK]
    cqn = cq * lax.rsqrt(jnp.mean(cq * cq, axis=-1, keepdims=True) + EPS) * qnw_ref[...]
    kvl = z1[:, Q_RANK:Q_RANK + KV_RANK]
    ckv_ref[...] = kvl * lax.rsqrt(jnp.mean(kvl * kvl, axis=-1, keepdims=True) + EPS) * g128[4:5]
    t1 = z1[:, 384:512]
    t2 = z1[:, 512:640]
    kinv = lax.rsqrt(jnp.sum(t1 * t1, axis=-1, keepdims=True) * (1.0 / ROPE) + EPS)
    kr = (t1 * g128[2:3] * tab[:, 256:384] + t2 * g128[3:4] * tab[:, 384:512]) * kinv
    kpe_ref[...] = kr[:, :ROPE]

    cqb = cqn.astype(BF16)
    qr = _dot(cqb, wuq_ref[...])
    qs = _dot(cqb, wuqs_ref[...])
    lane = lax.broadcasted_iota(jnp.int32, (1, LANES), 1)
    is_nope = lane < NOPE
    cos_f = tab[:, 0:128]
    sin_f = tab[:, 128:256]
    for hd in range(HEADS):
        t = qr[:, hd * LANES:(hd + 1) * LANES]
        sq = t * t
        ms_n = jnp.sum(jnp.where(is_nope, sq, 0.0), axis=-1, keepdims=True) * (1.0 / NOPE)
        ms_r = jnp.sum(jnp.where(is_nope, 0.0, sq), axis=-1, keepdims=True) * (1.0 / ROPE)
        inv = jnp.where(is_nope, lax.rsqrt(ms_n + EPS), lax.rsqrt(ms_r + EPS))
        qh = (t * g128[0:1] * cos_f + qs[:, hd * LANES:(hd + 1) * LANES] * g128[1:2] * sin_f) * (inv * scale)
        q_ref[hd] = qh.astype(BF16)

    ga_ref[...] = jax.nn.silu(_dot(hb, win_ref[:, C_GA:C_ZS]))
    gb_ref[...] = jax.nn.silu(_dot(hb, win_ref[:, C_GB:C_MA]))
    ma_ref[...] = jax.nn.sigmoid(_dot(hb, win_ref[:, C_MA:C_MB]))
    mb_ref[...] = jax.nn.sigmoid(_dot(hb, win_ref[:, C_MB:C_END]))

    zs = _dot(hb, win_ref[:, C_ZS:C_GB])
    rows = lax.broadcasted_iota(jnp.int32, (tm, 1), 0)
    prev = pltpu.roll(zs, 1, axis=0)
    if seq >= tm:
        tiles_per_seq = seq // tm

        @pl.when(i == 0)
        def _init_carry():
            carry_ref[...] = jnp.zeros(carry_ref.shape, F32)

        first = jnp.where(i % tiles_per_seq == 0, sh0_ref[0], carry_ref[...])
        prev = jnp.where(rows == 0, first, prev)
        carry_ref[...] = zs[tm - 1:tm, :]
        shift_ref[0] = zs[tm - 1:tm, :]
    else:
        for s in range(tm // seq):
            prev = jnp.where(rows == s * seq, sh0_ref[0, s:s + 1, :], prev)
            shift_ref[0, s:s + 1, :] = zs[(s + 1) * seq - 1:(s + 1) * seq, :]
    zm = zs + (prev - zs) * mu_ref[...]
    r = zm[:, 0:512]
    k = zm[:, 512:1024]
    v = zm[:, 1024:1536]
    t13 = zm[:, 1536:1664]
    v512 = v512_ref[...]
    w0, a0, k_k, k_a, r_k = v512[0:1], v512[1:2], v512[2:3], v512[3:4], v512[4:5]
    seg = seg_ref[...]
    log_w = -jax.nn.softplus(-(w0 + _mx(jnp.tanh(t13), w2_ref[...], 6))) - 0.5
    lw = -jnp.exp(log_w)
    aic = jax.nn.sigmoid(a0 + _mx(t13, a2_ref[...], 6))
    kkr = k * k_k
    kk = kkr / jnp.maximum(jnp.sqrt(_seg_sum(kkr * kkr, seg)), 1e-12)
    k2 = k * (1.0 + (aic - 1.0) * k_a)
    bon_ref[...] = _seg_sum(r * k2 * r_k, seg) * v
    rw_ref[:, 0:512] = r
    rw_ref[:, 512:1024] = lw
    rw_ref[:, 1024:1536] = k2
    rw_ref[:, 1536:2048] = v
    rw_ref[:, 2048:2560] = -kk
    rw_ref[:, 2560:3072] = kk * aic


def _in_stage(x2d, tab, sh0, lw, *, seq):
    n = x2d.shape[0]
    tm = 256 if seq >= 256 else min(256, n)
    assert n % tm == 0 and (seq % tm == 0 or tm % seq == 0)
    nseq = max(1, tm // seq)
    batch = n // seq
    sh0 = sh0.reshape(batch // nseq, nseq, SHIFT_WIDTH)
    if seq >= tm:
        sh_map = lambda i: (i // (seq // tm), 0, 0)
    else:
        sh_map = lambda i: (i, 0, 0)
    row = lambda w: pl.BlockSpec((tm, w), lambda i: (i, 0))
    out_shape = (
        jax.ShapeDtypeStruct((HEADS, n, LANES), BF16),
        jax.ShapeDtypeStruct((n, KV_RANK), F32),
        jax.ShapeDtypeStruct((n, ROPE), F32),
        jax.ShapeDtypeStruct((n, 6 * RW_WIDTH), F32),
        jax.ShapeDtypeStruct((n, RW_WIDTH), F32),
        jax.ShapeDtypeStruct((n, 512), F32),
        jax.ShapeDtypeStruct((n, 512), F32),
        jax.ShapeDtypeStruct((n, D_MODEL), F32),
        jax.ShapeDtypeStruct((n, D_MODEL), F32),
        jax.ShapeDtypeStruct((batch // nseq, nseq, SHIFT_WIDTH), F32),
    )
    out_specs = (
        pl.BlockSpec((HEADS, tm, LANES), lambda i: (0, i, 0)),
        row(KV_RANK), row(ROPE), row(6 * RW_WIDTH), row(RW_WIDTH), row(512), row(512), row(D_MODEL), row(D_MODEL),
        pl.BlockSpec((1, nseq, SHIFT_WIDTH), sh_map),
    )
    in_specs = [
        row(D_MODEL), row(512), pl.BlockSpec((1, nseq, SHIFT_WIDTH), sh_map),
        _const_spec((1, D_MODEL)),
        pl.BlockSpec((D_MODEL, C_END), lambda i: (0, 0), pipeline_mode=pl.Buffered(1)),
        _const_spec((1, Q_RANK)), _const_spec((Q_RANK, HEADS * LANES)), _const_spec((Q_RANK, HEADS * LANES)),
        _const_spec((8, LANES)), _const_spec((1, SHIFT_WIDTH)), _const_spec((8, RW_WIDTH)),
        _const_spec((LANES, RW_WIDTH)), _const_spec((LANES, RW_WIDTH)), _const_spec((RW_WIDTH, RW_WIDTH)),
    ]
    return pl.pallas_call(
        functools.partial(_in_kernel, tm=tm, seq=seq, scale=QK ** -0.5 * LOG2_E),
        grid=(n // tm,),
        in_specs=in_specs,
        out_specs=out_specs,
        out_shape=out_shape,
        scratch_shapes=[pltpu.VMEM((1, SHIFT_WIDTH), F32)],
        compiler_params=pltpu.CompilerParams(dimension_semantics=("arbitrary",), vmem_limit_bytes=VMEM_LIMIT),
        name="in_stage",
    )(x2d, tab, sh0, lw["norm_w"], lw["w_in"], lw["q_norm_w"], lw["w_uq"], lw["w_uq_sw"], lw["g128"], lw["mu"],
      lw["v512"], lw["w2"], lw["a2"], lw["seg"])


def _kv_kernel(ckv_ref, kpe_ref, wukv_ref, g128_ref, one_ref, k_ref, v_ref):
    kv = _dot(ckv_ref[...].astype(BF16), wukv_ref[...])
    kpe = kpe_ref[...]
    gkn = g128_ref[5:6]
    for hd in range(HEADS):
        kr = kv[:, hd * LANES:(hd + 1) * LANES]
        inv = lax.rsqrt(jnp.sum(kr * kr, axis=-1, keepdims=True) * (1.0 / NOPE) + EPS)
        k_ref[hd] = (kr * inv * gkn + kpe).astype(BF16)
        vr = kv[:, (HEADS + hd) * LANES:(HEADS + hd + 1) * LANES]
        v_ref[hd] = (vr + one_ref[hd % 2:hd % 2 + 1]).astype(BF16)


def _kv_stage(ckv2d, kpe2d, lw, *, tile):
    n = ckv2d.shape[0]
    assert n % tile == 0
    row = pl.BlockSpec((tile, LANES), lambda i: (i, 0))
    hspec = pl.BlockSpec((HEADS, tile, LANES), lambda i: (0, i, 0))
    return pl.pallas_call(
        _kv_kernel,
        grid=(n // tile,),
        in_specs=[row, row, _const_spec((KV_RANK, 2 * HEADS * LANES)), _const_spec((8, LANES)),
                  _const_spec((2, LANES))],
        out_specs=(hspec, hspec),
        out_shape=(jax.ShapeDtypeStruct((HEADS, n, LANES), BF16),) * 2,
        compiler_params=pltpu.CompilerParams(dimension_semantics=("parallel",), vmem_limit_bytes=VMEM_LIMIT),
        name="kv_stage",
    )(ckv2d, kpe2d, lw["w_ukv"], lw["g128"], lw["onecol"])


def _last_key_tile(qi, *, tq, tk, past):
    return (past + qi * tq + tq - 1) // tk


def _attn_kernel(qi_ref, kj_ref, q_ref, k_ref, v_ref, o_ref, m_scr, acc_scr, *, tq, tk, past, kv_len, nk):
    t = pl.program_id(1)
    qi = qi_ref[t]
    kj = kj_ref[t]

    @pl.when(kj == 0)
    def _init():
        m_scr[...] = jnp.full(m_scr.shape, NEG, F32)
        acc_scr[...] = jnp.zeros(acc_scr.shape, F32)

    q_first = past + qi * tq
    k_first = kj * tk
    k_last = k_first + tk - 1
    unmasked = jnp.logical_and((k_last >> CHUNK_SHIFT) <= (q_first >> CHUNK_SHIFT), k_last < kv_len)

    def compute(masked):
        if masked:
            rows = q_first + lax.broadcasted_iota(jnp.int32, (tq, tk), 0)
            cols = k_first + lax.broadcasted_iota(jnp.int32, (tq, tk), 1)
            ok = jnp.logical_and((cols >> CHUNK_SHIFT) <= (rows >> CHUNK_SHIFT), cols < kv_len)
        s_next = _dot_nt(q_ref[0], k_ref[0])
        for hd in range(HEADS):
            s = s_next
            if hd + 1 < HEADS:
                s_next = _dot_nt(q_ref[hd + 1], k_ref[hd + 1])
            if masked:
                s = jnp.where(ok, s, NEG)
            m_prev = m_scr[hd]
            m_new = jnp.maximum(m_prev, jnp.max(s, axis=-1, keepdims=True))
            alpha = jnp.exp2(m_prev - m_new)
            p = jnp.exp2(s - jnp.concatenate([m_new] * (tk // LANES), axis=1))
            acc_scr[hd] = acc_scr[hd] * alpha + _dot(p.astype(BF16), v_ref[hd])
            m_scr[hd] = m_new

    @pl.when(unmasked)
    def _full():
        compute(False)

    @pl.when(jnp.logical_not(unmasked))
    def _diag():
        compute(True)

    @pl.when(kj == jnp.minimum(_last_key_tile(qi, tq=tq, tk=tk, past=past), nk - 1))
    def _finish():
        lane = lax.broadcasted_iota(jnp.int32, (1, LANES), 1)
        for p in range(HEAD_PAIRS):
            ae = acc_scr[2 * p]
            ao = acc_scr[2 * p + 1]
            o_ref[:, p * LANES:(p + 1) * LANES] = jnp.where(lane < 64, ae / ae[:, 64:65], ao / ao[:, 0:1])


def _attn_stage(q, k, v, *, batch, seq, past, kv_len, kv_pad, tq, tk):
    nq = seq // tq
    nk = kv_pad // tk
    assert seq % tq == 0 and kv_pad % tk == 0 and tq % CHUNK == 0 and past % CHUNK == 0
    steps = [(qi, kj) for qi in range(nq)
             for kj in range(min(_last_key_tile(qi, tq=tq, tk=tk, past=past), nk - 1) + 1)]
    qi_tab = jnp.asarray([s[0] for s in steps], jnp.int32)
    kj_tab = jnp.asarray([s[1] for s in steps], jnp.int32)
    kv_map = lambda b, t, qi, kj: (0, b * nk + kj[t], 0)
    grid_spec = pltpu.PrefetchScalarGridSpec(
        num_scalar_prefetch=2,
        grid=(batch, len(steps)),
        in_specs=[pl.BlockSpec((HEADS, tq, LANES), lambda b, t, qi, kj: (0, b * nq + qi[t], 0)),
                  pl.BlockSpec((HEADS, tk, LANES), kv_map),
                  pl.BlockSpec((HEADS, tk, LANES), kv_map)],
        out_specs=pl.BlockSpec((tq, HEADS * NOPE), lambda b, t, qi, kj: (b * nq + qi[t], 0)),
        scratch_shapes=[pltpu.VMEM((HEADS, tq, LANES), F32), pltpu.VMEM((HEADS, tq, LANES), F32)],
    )
    return pl.pallas_call(
        functools.partial(_attn_kernel, tq=tq, tk=tk, past=past, kv_len=kv_len, nk=nk),
        grid_spec=grid_spec,
        out_shape=jax.ShapeDtypeStruct((batch * seq, HEADS * NOPE), F32),
        compiler_params=pltpu.CompilerParams(dimension_semantics=("parallel", "arbitrary"),
                                             vmem_limit_bytes=VMEM_LIMIT),
        name="attention",
    )(qi_tab, kj_tab, q, k, v)


def _wkv_kernel(rw_ref, m0_ref, tri_ref, y_ref, mout_ref, m_scr, *, nchunks):
    j = pl.program_id(1)

    @pl.when(j == 0)
    def _load():
        m_scr[...] = m0_ref[0]

    L = CHUNK
    r = rw_ref[:, 0:512]
    lw = rw_ref[:, 512:1024]
    k = rw_ref[:, 1024:1536]
    v = rw_ref[:, 1536:2048]
    a = rw_ref[:, 2048:2560]
    b = rw_ref[:, 2560:3072]
    c = _mx(tri_ref[...], lw, 6)
    gt = c[L - 1:L, :]
    en = jnp.exp(-c)
    eh = jnp.exp(gt - c)
    rt = r * jnp.exp(c)
    at = a * jnp.exp(c - lw)
    kt = k * en
    bt = b * en
    kh = k * eh
    bh = b * eh
    egt = jnp.exp(gt)

    lane = lax.broadcasted_iota(jnp.int32, (L, LANES), 1)
    even = lane < RW_HEAD
    ri = lax.broadcasted_iota(jnp.int32, (LANES, LANES), 0)
    ci = lax.broadcasted_iota(jnp.int32, (LANES, LANES), 1)
    tr = ri & (L - 1)
    tc = ci & (L - 1)
    strict = tr > tc
    incl = tr >= tc
    blk16 = (ri >> 4) == (ci >> 4)
    eye = ri == ci
    ident = jnp.where(eye, 1.0, 0.0).astype(F32)

    def stack(t):
        return jnp.concatenate([jnp.where(even, t, 0.0), jnp.where(even, 0.0, t)], axis=0)

    pairs = range(HEAD_PAIRS)
    each = lambda fn, *cols: [fn(*args) for args in zip(*cols)]
    cat0 = lambda *xs: jnp.concatenate(xs, axis=0)
    cat1 = lambda *xs: jnp.concatenate(xs, axis=1)
    sls = [slice(p * LANES, (p + 1) * LANES) for p in pairs]
    a_s, r_s, b_s, k_s, bh_s, kh_s, v_s = [[stack(t[:, sl]) for sl in sls] for t in (at, rt, bt, kt, bh, kh, v)]
    m = [m_scr[p] for p in pairs]
    prod = each(lambda a_, r_, b_, k_: _mx(cat0(a_, r_), cat0(b_, k_), P_PROD, _NT), a_s, r_s, b_s, k_s)
    n_ab = [jnp.where(strict, x[0:LANES, 0:LANES], 0.0) for x in prod]
    a_ak = [jnp.where(strict, x[0:LANES, LANES:2 * LANES], 0.0) for x in prod]
    a_rb = [jnp.where(incl, x[LANES:2 * LANES, 0:LANES], 0.0) for x in prod]
    a_rk = [jnp.where(incl, x[LANES:2 * LANES, LANES:2 * LANES], 0.0) for x in prod]
    inv = lambda x, y_: _mx(x, y_, P_INV)
    d1 = [jnp.where(blk16, x, 0.0) for x in n_ab]
    e1 = each(lambda x, d: x - d, n_ab, d1)
    p1 = [ident + d for d in d1]
    d2 = each(inv, d1, d1)
    p2 = each(lambda p_, d: p_ + inv(p_, d), p1, d2)
    d4 = each(inv, d2, d2)
    p3 = each(lambda p_, d: p_ + inv(p_, d), p2, d4)
    d8 = each(inv, d4, d4)
    t_d = each(lambda p_, d: p_ + inv(p_, d), p3, d8)
    f1 = each(inv, t_d, e1)
    f2 = each(inv, f1, f1)
    if1 = [ident + f for f in f1]
    g1 = each(lambda i_, f: i_ + inv(i_, f), if1, f2)
    t_m = each(inv, g1, t_d)
    w = each(lambda a_, ak, m_, v_: _mx(cat1(a_, ak), cat0(m_, v_), P_W), a_s, a_ak, m, v_s)
    u = each(lambda t_, w_: _mx(t_, w_, P_U), t_m, w)
    y = each(lambda r_, rb, rk, m_, u_, v_: _mx(cat1(r_, rb, rk), cat0(m_, u_, v_), P_Y), r_s, a_rb, a_rk, m, u, v_s)
    gcol = [jnp.transpose(jnp.broadcast_to(egt[:, sl], (LANES, LANES))) for sl in sls]
    m_new = each(lambda b_, k_, u_, v_, m_, g_: m_ * g_ + _mx(cat0(b_, k_), cat0(u_, v_), P_M, _TN),
                 bh_s, kh_s, u, v_s, m, gcol)
    for p in pairs:
        y_ref[:, sls[p]] = y[p][0:L] + y[p][L:2 * L]
        m_scr[p] = m_new[p]

    @pl.when(j == nchunks - 1)
    def _store():
        mout_ref[0] = m_scr[...]


def _wkv_stage(rw, m0, tri, *, batch, seq):
    nchunks = seq // CHUNK
    assert seq % CHUNK == 0
    return pl.pallas_call(
        functools.partial(_wkv_kernel, nchunks=nchunks),
        grid=(batch, nchunks),
        in_specs=[pl.BlockSpec((CHUNK, 6 * RW_WIDTH), lambda b, j: (b * nchunks + j, 0)),
                  pl.BlockSpec((1, HEAD_PAIRS, LANES, LANES), lambda b, j: (b, 0, 0, 0)),
                  _const_spec((CHUNK, CHUNK))],
        out_specs=(pl.BlockSpec((CHUNK, RW_WIDTH), lambda b, j: (b * nchunks + j, 0)),
                   pl.BlockSpec((1, HEAD_PAIRS, LANES, LANES), lambda b, j: (b, 0, 0, 0))),
        out_shape=(jax.ShapeDtypeStruct((batch * seq, RW_WIDTH), F32),
                   jax.ShapeDtypeStruct((batch, HEAD_PAIRS, LANES, LANES), F32)),
        scratch_shapes=[pltpu.VMEM((HEAD_PAIRS, LANES, LANES), F32)],
        compiler_params=pltpu.CompilerParams(dimension_semantics=("parallel", "arbitrary"),
                                             vmem_limit_bytes=VMEM_LIMIT),
        name="wkv",
    )(rw, m0, tri)


def _out_kernel(x_ref, oa_ref, y_ref, bon_ref, ga_ref, gb_ref, ma_ref, mb_ref, v512_ref, seg_ref,
                woa_ref, wob_ref, wo_ref, out_ref):
    v512 = v512_ref[...]
    lnx_w, lnx_b = v512[5:6], v512[6:7]
    seg = seg_ref[...]
    oa = (oa_ref[...] * ga_ref[...]).astype(BF16)
    y = y_ref[...]
    d = y - _seg_sum(y, seg) * (1.0 / RW_HEAD)
    var = _seg_sum(d * d, seg) * (1.0 / RW_HEAD)
    yn = d * lax.rsqrt(var + GN_EPS) * lnx_w + lnx_b
    ob = ((yn + bon_ref[...]) * gb_ref[...]).astype(BF16)
    m = ma_ref[...] * _dot(oa, woa_ref[...]) + mb_ref[...] * _dot(ob, wob_ref[...])
    out_ref[...] = x_ref[...] + _dot(m.astype(BF16), wo_ref[...])


def _out_stage(x2d, oa, y, bon, ga, gb, ma, mb, lw):
    n = x2d.shape[0]
    tm = min(256, n)
    assert n % tm == 0
    row = lambda w: pl.BlockSpec((tm, w), lambda i: (i, 0))
    return pl.pallas_call(
        _out_kernel,
        grid=(n // tm,),
        in_specs=[row(D_MODEL), row(512), row(512), row(512), row(512), row(512), row(D_MODEL), row(D_MODEL),
                  _const_spec((8, RW_WIDTH)), _const_spec((RW_WIDTH, RW_WIDTH)),
                  _const_spec((512, D_MODEL)), _const_spec((512, D_MODEL)), _const_spec((D_MODEL, D_MODEL))],
        out_specs=row(D_MODEL),
        out_shape=jax.ShapeDtypeStruct((n, D_MODEL), F32),
        compiler_params=pltpu.CompilerParams(dimension_semantics=("parallel",), vmem_limit_bytes=VMEM_LIMIT),
        name="out_stage",
    )(x2d, oa, y, bon, ga, gb, ma, mb, lw["v512"], lw["seg"], lw["w_out_a"], lw["w_out_b"], lw["w_o"])


def _swap_halves(w):
    half = ROPE // 2
    return jnp.concatenate([w[..., half:], w[..., :half]], axis=-1)


def _prep_weights(norm_w, w_in, q_norm_w, kv_norm_w, w_uq, w_ukv, qn_nope, qn_rope, kn_nope, kn_rope,
                  mu_shift, w0, w2, a0, a2, k_k, k_a, r_k, lnx_w, lnx_b, w_out_a, w_out_b, w_o):
    depth = w_in.shape[0]
    z = lambda *s: jnp.zeros((depth,) + s, F32)
    kpe = w_in[..., 384:416]
    w_in_p = jnp.concatenate([w_in[..., :384], kpe, z(D_MODEL, 96), _swap_halves(kpe), z(D_MODEL, 96),
                              w_in[..., 416:]], axis=-1).astype(BF16)
    uq = w_uq.reshape(depth, Q_RANK, HEADS, QK)
    uq_p = jnp.concatenate([uq, z(Q_RANK, HEADS, 32)], axis=-1).reshape(depth, Q_RANK, HEADS * LANES)
    uq_s = jnp.concatenate([z(Q_RANK, HEADS, NOPE), _swap_halves(uq[..., NOPE:]), z(Q_RANK, HEADS, 32)],
                           axis=-1).reshape(depth, Q_RANK, HEADS * LANES)
    ukv = w_ukv.reshape(depth, KV_RANK, HEADS, 2 * NOPE)
    zk = z(KV_RANK, HEADS, NOPE)
    k_part = jnp.concatenate([ukv[..., :NOPE], zk], axis=-1)
    vv = ukv[..., NOPE:]
    odd = (jnp.arange(HEADS) % 2 == 1)[None, None, :, None]
    v_part = jnp.where(odd, jnp.concatenate([zk, vv], axis=-1), jnp.concatenate([vv, zk], axis=-1))
    ukv_p = jnp.concatenate([k_part.reshape(depth, KV_RANK, HEADS * LANES),
                             v_part.reshape(depth, KV_RANK, HEADS * LANES)], axis=-1).astype(BF16)
    zg = lambda n: jnp.zeros((depth, n), F32)
    g128 = jnp.stack([
        jnp.concatenate([qn_nope, qn_rope, zg(32)], axis=-1),
        jnp.concatenate([zg(NOPE), _swap_halves(qn_rope), zg(32)], axis=-1),
        jnp.concatenate([kn_rope, zg(96)], axis=-1),
        jnp.concatenate([_swap_halves(kn_rope), zg(96)], axis=-1),
        kv_norm_w,
        jnp.concatenate([kn_nope, zg(NOPE)], axis=-1),
        zg(LANES), zg(LANES)], axis=1)
    v512 = jnp.stack([w0, a0, k_k, k_a, r_k.reshape(depth, RW_WIDTH), lnx_w, lnx_b, zg(RW_WIDTH)], axis=1)
    seg_id = jnp.arange(RW_WIDTH) // RW_HEAD
    seg = (seg_id[:, None] == seg_id[None, :]).astype(BF16)
    onecol = jnp.stack([jnp.arange(LANES) == NOPE, jnp.arange(LANES) == 0]).astype(F32)
    return dict(
        norm_w=norm_w[:, None, :], w_in=w_in_p, q_norm_w=q_norm_w[:, None, :],
        w_uq=uq_p.astype(BF16), w_uq_sw=uq_s.astype(BF16), w_ukv=ukv_p, g128=g128, mu=mu_shift[:, None, :],
        v512=v512, w2=jnp.concatenate([w2, z(LORA, RW_WIDTH)], axis=1),
        a2=jnp.concatenate([z(LORA, RW_WIDTH), a2], axis=1),
        w_out_a=w_out_a.astype(BF16), w_out_b=w_out_b.astype(BF16), w_o=w_o.astype(BF16),
    ), dict(seg=seg, onecol=onecol)


def _rope_table(past, seq, batch):
    half = ROPE // 2
    inv = ROPE_THETA ** (-jnp.arange(half, dtype=F32) / half)
    pos = past + jnp.arange(seq, dtype=jnp.int32)
    ang = pos.astype(F32)[:, None] * inv[None, :]
    cos, sin = jnp.cos(ang), jnp.sin(ang)
    one, zero = jnp.ones((seq, NOPE), F32), jnp.zeros((seq, NOPE), F32)
    z32, z96 = jnp.zeros((seq, 32), F32), jnp.zeros((seq, 96), F32)
    tab = jnp.concatenate([one, cos, cos, z32, zero, -sin, sin, z32, cos, cos, z96, -sin, sin, z96], axis=-1)
    return jnp.tile(tab, (batch, 1))


def _state_to_pairs(s):
    b = s.shape[0]
    m = jnp.swapaxes(s.astype(F32), -1, -2).reshape(b, HEAD_PAIRS, 2, RW_HEAD, RW_HEAD)
    zero = jnp.zeros_like(m[:, :, 0])
    top = jnp.concatenate([m[:, :, 0], zero], axis=-1)
    bot = jnp.concatenate([zero, m[:, :, 1]], axis=-1)
    return jnp.concatenate([top, bot], axis=-2)


def _pairs_to_state(m):
    b = m.shape[0]
    e = m[:, :, :RW_HEAD, :RW_HEAD]
    o = m[:, :, RW_HEAD:, RW_HEAD:]
    return jnp.swapaxes(jnp.stack([e, o], axis=2).reshape(b, HEADS, RW_HEAD, RW_HEAD), -1, -2)


def _round_up(n, m):
    return (n + m - 1) // m * m


def _group_layer(x2d, tab, lw, ckv_past, kpe_past, m0, sh0, tri, *, batch, seq, past):
    n = batch * seq
    q, ckv, kpe, rw, bon, ga, gb, ma, mb, shift = _in_stage(x2d, tab, sh0, lw, seq=seq)
    ckv3 = ckv.reshape(batch, seq, KV_RANK)
    kpe3 = kpe.reshape(batch, seq, ROPE)
    kv_len = past + seq
    if past > 0:
        ckv_all = jnp.concatenate([ckv_past, ckv3], axis=1)
        kpe_all = jnp.concatenate([kpe_past, kpe3], axis=1)
    else:
        ckv_all, kpe_all = ckv3, kpe3
    if seq >= 512:
        tq, tk = 512, 512
        kv_pad = _round_up(kv_len, tk)
    else:
        tq = seq
        kv_pad = _round_up(kv_len, LANES)
        tk = kv_pad
    ckv_all = jnp.pad(ckv_all, ((0, 0), (0, kv_pad - kv_len), (0, 0)))
    kpe_all = jnp.pad(kpe_all, ((0, 0), (0, kv_pad - kv_len), (NOPE, LANES - QK)))
    kv_tile = max(t for t in (128, 256, 384, 512) if kv_pad % t == 0)
    k_att, v_att = _kv_stage(ckv_all.reshape(batch * kv_pad, KV_RANK), kpe_all.reshape(batch * kv_pad, LANES),
                             lw, tile=kv_tile)
    oa = _attn_stage(q, k_att, v_att, batch=batch, seq=seq, past=past, kv_len=kv_len, kv_pad=kv_pad, tq=tq, tk=tk)
    y, m1 = _wkv_stage(rw, m0, tri, batch=batch, seq=seq)
    out = _out_stage(x2d, oa, y, bon, ga, gb, ma, mb, lw)
    return out, ckv3, kpe3, _pairs_to_state(m1), shift.reshape(batch, 1, SHIFT_WIDTH)


def kernel(x_prompt, x_sample, cache_ckv, cache_kpe, state_wkv, state_shift, norm_w, w_in, q_norm_w, kv_norm_w,
           w_uq, w_ukv, qn_nope, qn_rope, kn_nope, kn_rope, mu_shift, w0, w2, a0, a2, k_k, k_a, r_k, lnx_w, lnx_b,
           w_out_a, w_out_b, w_o):
    depth = w_in.shape[0]
    bp, tp, _ = x_prompt.shape
    bs, ts, _ = x_sample.shape
    past = cache_ckv.shape[2]
    assert past % CHUNK == 0 and tp % CHUNK == 0 and ts % CHUNK == 0
    stacked, shared = _prep_weights(norm_w, w_in, q_norm_w, kv_norm_w, w_uq, w_ukv, qn_nope, qn_rope, kn_nope,
                                    kn_rope, mu_shift, w0, w2, a0, a2, k_k, k_a, r_k, lnx_w, lnx_b,
                                    w_out_a, w_out_b, w_o)
    tab_p = _rope_table(0, tp, bp)
    tab_s = _rope_table(past, ts, bs)
    tri = (jnp.arange(CHUNK)[:, None] >= jnp.arange(CHUNK)[None, :]).astype(F32)
    zero_m = jnp.zeros((bp, HEAD_PAIRS, LANES, LANES), F32)
    zero_sh = jnp.zeros((bp, 1, SHIFT_WIDTH), F32)

    yp = x_prompt.reshape(bp * tp, D_MODEL)
    ys = x_sample.reshape(bs * ts, D_MODEL)
    outs_p, outs_s = [], []
    for i in range(depth):
        lw = {name: val[i] for name, val in stacked.items()}
        lw.update(shared)
        yp, *rest_p = _group_layer(yp, tab_p, lw, None, None, zero_m, zero_sh, tri, batch=bp, seq=tp, past=0)
        outs_p.append(rest_p)
        ys, *rest_s = _group_layer(ys, tab_s, lw, cache_ckv[i], cache_kpe[i], _state_to_pairs(state_wkv[i]),
                                   state_shift[i], tri, batch=bs, seq=ts, past=past)
        outs_s.append(rest_s)
    stack = lambda outs, j: jnp.stack([o[j] for o in outs])
    return (yp.reshape(bp, tp, D_MODEL), ys.reshape(bs, ts, D_MODEL),
            stack(outs_p, 0), stack(outs_p, 1), stack(outs_p, 2), stack(outs_p, 3),
            stack(outs_s, 0), stack(outs_s, 1), stack(outs_s, 2), stack(outs_s, 3))
```

```python
import functools

import jax
import jax.numpy as jnp
from jax import lax
from jax.experimental import pallas as pl
from jax.experimental.pallas import tpu as pltpu

F32 = jnp.float32
BF16 = jnp.bfloat16
HI = lax.Precision.HIGHEST

D_MODEL = 1024
CHUNK = 64
CHUNK_SHIFT = 6
EPS = 1e-6
GN_EPS = 64e-5
HEADS = 8
NOPE = 64
ROPE = 32
QK = NOPE + ROPE
Q_RANK = 256
KV_RANK = 128
ROPE_THETA = 10000.0
RW_WIDTH = 512
RW_HEAD = 64
LORA = 64
SHIFT_WIDTH = 3 * RW_WIDTH + 2 * LORA
LANES = 128
HEAD_PAIRS = HEADS // 2
NEG = -1e30
P_PROD, P_INV, P_W, P_U, P_Y, P_M = 1, 1, 1, 1, 1, 1
LOG2_E = 1.4426950408889634

C_LAT = 0
C_GA = 640
C_ZS = C_GA + 512
C_GB = C_ZS + SHIFT_WIDTH
C_MA = C_GB + 512
C_MB = C_MA + D_MODEL
C_END = C_MB + D_MODEL

VMEM_LIMIT = 56 * 1024 * 1024


def _const_spec(shape):
    n = len(shape)
    return pl.BlockSpec(shape, lambda *_: (0,) * n)


_NN = (((1,), (0,)), ((), ()))
_NT = (((1,), (1,)), ((), ()))
_TN = (((0,), (0,)), ((), ()))


def _dot(a, b):
    return jnp.dot(a, b, preferred_element_type=F32)


def _dot_nt(a, b):
    return lax.dot_general(a, b, _NT, preferred_element_type=F32)


def _split(x):
    hi = x.astype(BF16)
    return hi, (x - hi.astype(F32)).astype(BF16)


def _mx(a, b, passes, dims=_NN):
    if passes == 6:
        return lax.dot_general(a, b, dims, precision=HI, preferred_element_type=F32)
    dg = lambda x, y: lax.dot_general(x, y, dims, preferred_element_type=F32)
    if passes == 1:
        return dg(a.astype(BF16), b.astype(BF16))
    ah, al = _split(a)
    bh, bl = _split(b)
    return dg(ah, bh) + (dg(ah, bl) + dg(al, bh))


def _seg_sum(x, seg, passes=2):
    hi = x.astype(BF16)
    if passes == 1:
        return _dot(hi, seg)
    lo = (x - hi.astype(F32)).astype(BF16)
    return _dot(hi, seg) + _dot(lo, seg)


def _in_kernel(x_ref, tab_ref, sh0_ref, normw_ref, win_ref, qnw_ref, wuq_ref, wuqs_ref, g128_ref, mu_ref,
               v512_ref, w2_ref, a2_ref, seg_ref,
               q_ref, ckv_ref, kpe_ref, rw_ref, bon_ref, ga_ref, gb_ref, ma_ref, mb_ref, shift_ref,
               carry_ref, *, tm, seq, scale):
    i = pl.program_id(0)
    x = x_ref[...]
    h = x * lax.rsqrt(jnp.mean(x * x, axis=-1, keepdims=True) + EPS) * normw_ref[...]
    hb = h.astype(BF16)
    tab = tab_ref[...]
    g128 = g128_ref[...]

    z1 = _dot(hb, win_ref[:, C_LAT:C_GA])
    cq = z1[:, :Q_RANK]
    cqn = cq * lax.rsqrt(jnp.mean(cq * cq, axis=-1, keepdims=True) + EPS) * qnw_ref[...]
    kvl = z1[:, Q_RANK:Q_RANK + KV_RANK]
    ckv_ref[...] = kvl * lax.rsqrt(jnp.mean(kvl * kvl, axis=-1, keepdims=True) + EPS) * g128[4:5]
    t1 = z1[:, 384:512]
    t2 = z1[:, 512:640]
    kinv = lax.rsqrt(jnp.sum(t1 * t1, axis=-1, keepdims=True) * (1.0 / ROPE) + EPS)
    kr = (t1 * g128[2:3] * tab[:, 256:384] + t2 * g128[3:4] * tab[:, 384:512]) * kinv
    kpe_ref[...] = kr[:, :ROPE]

    cqb = cqn.astype(BF16)
    qr = _dot(cqb, wuq_ref[...])
    qs = _dot(cqb, wuqs_ref[...])
    lane = lax.broadcasted_iota(jnp.int32, (1, LANES), 1)
    is_nope = lane < NOPE
    cos_f = tab[:, 0:128]
    sin_f = tab[:, 128:256]
    for hd in range(HEADS):
        t = qr[:, hd * LANES:(hd + 1) * LANES]
        sq = t * t
        ms_n = jnp.sum(jnp.where(is_nope, sq, 0.0), axis=-1, keepdims=True) * (1.0 / NOPE)
        ms_r = jnp.sum(jnp.where(is_nope, 0.0, sq), axis=-1, keepdims=True) * (1.0 / ROPE)
        inv = jnp.where(is_nope, lax.rsqrt(ms_n + EPS), lax.rsqrt(ms_r + EPS))
        qh = (t * g128[0:1] * cos_f + qs[:, hd * LANES:(hd + 1) * LANES] * g128[1:2] * sin_f) * (inv * scale)
        q_ref[hd] = qh.astype(BF16)

    ga_ref[...] = jax.nn.silu(_dot(hb, win_ref[:, C_GA:C_ZS]))
    gb_ref[...] = jax.nn.silu(_dot(hb, win_ref[:, C_GB:C_MA]))
    ma_ref[...] = jax.nn.sigmoid(_dot(hb, win_ref[:, C_MA:C_MB]))
    mb_ref[...] = jax.nn.sigmoid(_dot(hb, win_ref[:, C_MB:C_END]))

    zs = _dot(hb, win_ref[:, C_ZS:C_GB])
    rows = lax.broadcasted_iota(jnp.int32, (tm, 1), 0)
    prev = pltpu.roll(zs, 1, axis=0)
    if seq >= tm:
        tiles_per_seq = seq // tm

        @pl.when(i == 0)
        def _init_carry():
            carry_ref[...] = jnp.zeros(carry_ref.shape, F32)

        first = jnp.where(i % tiles_per_seq == 0, sh0_ref[0], carry_ref[...])
        prev = jnp.where(rows == 0, first, prev)
        carry_ref[...] = zs[tm - 1:tm, :]
        shift_ref[0] = zs[tm - 1:tm, :]
    else:
        for s in range(tm // seq):
            prev = jnp.where(rows == s * seq, sh0_ref[0, s:s + 1, :], prev)
            shift_ref[0, s:s + 1, :] = zs[(s + 1) * seq - 1:(s + 1) * seq, :]
    zm = zs + (prev - zs) * mu_ref[...]
    r = zm[:, 0:512]
    k = zm[:, 512:1024]
    v = zm[:, 1024:1536]
    t13 = zm[:, 1536:1664]
    v512 = v512_ref[...]
    w0, a0, k_k, k_a, r_k = v512[0:1], v512[1:2], v512[2:3], v512[3:4], v512[4:5]
    seg = seg_ref[...]
    log_w = -jax.nn.softplus(-(w0 + _mx(jnp.tanh(t13), w2_ref[...], 3))) - 0.5
    lw = -jnp.exp(log_w)
    aic = jax.nn.sigmoid(a0 + _mx(t13, a2_ref[...], 3))
    kkr = k * k_k
    kk = kkr / jnp.maximum(jnp.sqrt(_seg_sum(kkr * kkr, seg, 1)), 1e-12)
    k2 = k * (1.0 + (aic - 1.0) * k_a)
    bon_ref[...] = _seg_sum(r * k2 * r_k, seg, 1) * v
    rw_ref[:, 0:512] = r
    rw_ref[:, 512:1024] = lw
    rw_ref[:, 1024:1536] = k2
    rw_ref[:, 1536:2048] = v
    rw_ref[:, 2048:2560] = -kk
    rw_ref[:, 2560:3072] = kk * aic


def _in_stage(x2d, tab, sh0, lw, *, seq):
    n = x2d.shape[0]
    tm = 256 if seq >= 256 else min(256, n)
    assert n % tm == 0 and (seq % tm == 0 or tm % seq == 0)
    nseq = max(1, tm // seq)
    batch = n // seq
    sh0 = sh0.reshape(batch // nseq, nseq, SHIFT_WIDTH)
    if seq >= tm:
        sh_map = lambda i: (i // (seq // tm), 0, 0)
    else:
        sh_map = lambda i: (i, 0, 0)
    row = lambda w: pl.BlockSpec((tm, w), lambda i: (i, 0))
    out_shape = (
        jax.ShapeDtypeStruct((HEADS, n, LANES), BF16),
        jax.ShapeDtypeStruct((n, KV_RANK), F32),
        jax.ShapeDtypeStruct((n, ROPE), F32),
        jax.ShapeDtypeStruct((n, 6 * RW_WIDTH), F32),
        jax.ShapeDtypeStruct((n, RW_WIDTH), F32),
        jax.ShapeDtypeStruct((n, 512), F32),
        jax.ShapeDtypeStruct((n, 512), F32),
        jax.ShapeDtypeStruct((n, D_MODEL), F32),
        jax.ShapeDtypeStruct((n, D_MODEL), F32),
        jax.ShapeDtypeStruct((batch // nseq, nseq, SHIFT_WIDTH), F32),
    )
    out_specs = (
        pl.BlockSpec((HEADS, tm, LANES), lambda i: (0, i, 0)),
        row(KV_RANK), row(ROPE), row(6 * RW_WIDTH), row(RW_WIDTH), row(512), row(512), row(D_MODEL), row(D_MODEL),
        pl.BlockSpec((1, nseq, SHIFT_WIDTH), sh_map),
    )
    in_specs = [
        row(D_MODEL), row(512), pl.BlockSpec((1, nseq, SHIFT_WIDTH), sh_map),
        _const_spec((1, D_MODEL)),
        pl.BlockSpec((D_MODEL, C_END), lambda i: (0, 0), pipeline_mode=pl.Buffered(1)),
        _const_spec((1, Q_RANK)), _const_spec((Q_RANK, HEADS * LANES)), _const_spec((Q_RANK, HEADS * LANES)),
        _const_spec((8, LANES)), _const_spec((1, SHIFT_WIDTH)), _const_spec((8, RW_WIDTH)),
        _const_spec((LANES, RW_WIDTH)), _const_spec((LANES, RW_WIDTH)), _const_spec((RW_WIDTH, RW_WIDTH)),
    ]
    return pl.pallas_call(
        functools.partial(_in_kernel, tm=tm, seq=seq, scale=QK ** -0.5 * LOG2_E),
        grid=(n // tm,),
        in_specs=in_specs,
        out_specs=out_specs,
        out_shape=out_shape,
        scratch_shapes=[pltpu.VMEM((1, SHIFT_WIDTH), F32)],
        compiler_params=pltpu.CompilerParams(dimension_semantics=("arbitrary",), vmem_limit_bytes=VMEM_LIMIT),
        name="in_stage",
    )(x2d, tab, sh0, lw["norm_w"], lw["w_in"], lw["q_norm_w"], lw["w_uq"], lw["w_uq_sw"], lw["g128"], lw["mu"],
      lw["v512"], lw["w2"], lw["a2"], lw["seg"])


def _kv_kernel(ckv_ref, kpe_ref, wukv_ref, g128_ref, one_ref, k_ref, v_ref):
    kv = _dot(ckv_ref[...].astype(BF16), wukv_ref[...])
    kpe = kpe_ref[...]
    gkn = g128_ref[5:6]
    for hd in range(HEADS):
        kr = kv[:, hd * LANES:(hd + 1) * LANES]
        inv = lax.rsqrt(jnp.sum(kr * kr, axis=-1, keepdims=True) * (1.0 / NOPE) + EPS)
        k_ref[hd] = (kr * inv * gkn + kpe).astype(BF16)
        vr = kv[:, (HEADS + hd) * LANES:(HEADS + hd + 1) * LANES]
        v_ref[hd] = (vr + one_ref[hd % 2:hd % 2 + 1]).astype(BF16)


def _kv_stage(ckv2d, kpe2d, lw, *, tile):
    n = ckv2d.shape[0]
    assert n % tile == 0
    row = pl.BlockSpec((tile, LANES), lambda i: (i, 0))
    hspec = pl.BlockSpec((HEADS, tile, LANES), lambda i: (0, i, 0))
    return pl.pallas_call(
        _kv_kernel,
        grid=(n // tile,),
        in_specs=[row, row, _const_spec((KV_RANK, 2 * HEADS * LANES)), _const_spec((8, LANES)),
                  _const_spec((2, LANES))],
        out_specs=(hspec, hspec),
        out_shape=(jax.ShapeDtypeStruct((HEADS, n, LANES), BF16),) * 2,
        compiler_params=pltpu.CompilerParams(dimension_semantics=("parallel",), vmem_limit_bytes=VMEM_LIMIT),
        name="kv_stage",
    )(ckv2d, kpe2d, lw["w_ukv"], lw["g128"], lw["onecol"])


def _last_key_tile(qi, *, tq, tk, past):
    return (past + qi * tq + tq - 1) // tk


def _attn_kernel(qi_ref, kj_ref, q_ref, k_ref, v_ref, o_ref, m_scr, acc_scr, *, tq, tk, past, kv_len, nk):
    t = pl.program_id(1)
    qi = qi_ref[t]
    kj = kj_ref[t]

    @pl.when(kj == 0)
    def _init():
        m_scr[...] = jnp.full(m_scr.shape, NEG, F32)
        acc_scr[...] = jnp.zeros(acc_scr.shape, F32)

    q_first = past + qi * tq
    k_first = kj * tk
    k_last = k_first + tk - 1
    unmasked = jnp.logical_and((k_last >> CHUNK_SHIFT) <= (q_first >> CHUNK_SHIFT), k_last < kv_len)

    def compute(masked):
        if masked:
            rows = q_first + lax.broadcasted_iota(jnp.int32, (tq, tk), 0)
            cols = k_first + lax.broadcasted_iota(jnp.int32, (tq, tk), 1)
            ok = jnp.logical_and((cols >> CHUNK_SHIFT) <= (rows >> CHUNK_SHIFT), cols < kv_len)
        s_next = _dot_nt(q_ref[0], k_ref[0])
        for hd in range(HEADS):
            s = s_next
            if hd + 1 < HEADS:
                s_next = _dot_nt(q_ref[hd + 1], k_ref[hd + 1])
            if masked:
                s = jnp.where(ok, s, NEG)
            m_prev = m_scr[hd]
            m_new = jnp.maximum(m_prev, jnp.max(s, axis=-1, keepdims=True))
            alpha = jnp.exp2(m_prev - m_new)
            p = jnp.exp2((s - jnp.concatenate([m_new] * (tk // LANES), axis=1)).astype(BF16))
            acc_scr[hd] = acc_scr[hd] * alpha + _dot(p, v_ref[hd])
            m_scr[hd] = m_new

    @pl.when(unmasked)
    def _full():
        compute(False)

    @pl.when(jnp.logical_not(unmasked))
    def _diag():
        compute(True)

    @pl.when(kj == jnp.minimum(_last_key_tile(qi, tq=tq, tk=tk, past=past), nk - 1))
    def _finish():
        lane = lax.broadcasted_iota(jnp.int32, (1, LANES), 1)
        for p in range(HEAD_PAIRS):
            ae = acc_scr[2 * p]
            ao = acc_scr[2 * p + 1]
            o_ref[:, p * LANES:(p + 1) * LANES] = jnp.where(lane < 64, ae / ae[:, 64:65], ao / ao[:, 0:1])


def _attn_stage(q, k, v, *, batch, seq, past, kv_len, kv_pad, tq, tk):
    nq = seq // tq
    nk = kv_pad // tk
    assert seq % tq == 0 and kv_pad % tk == 0 and tq % CHUNK == 0 and past % CHUNK == 0
    steps = [(qi, kj) for qi in range(nq)
             for kj in range(min(_last_key_tile(qi, tq=tq, tk=tk, past=past), nk - 1) + 1)]
    qi_tab = jnp.asarray([s[0] for s in steps], jnp.int32)
    kj_tab = jnp.asarray([s[1] for s in steps], jnp.int32)
    kv_map = lambda b, t, qi, kj: (0, b * nk + kj[t], 0)
    grid_spec = pltpu.PrefetchScalarGridSpec(
        num_scalar_prefetch=2,
        grid=(batch, len(steps)),
        in_specs=[pl.BlockSpec((HEADS, tq, LANES), lambda b, t, qi, kj: (0, b * nq + qi[t], 0)),
                  pl.BlockSpec((HEADS, tk, LANES), kv_map),
                  pl.BlockSpec((HEADS, tk, LANES), kv_map)],
        out_specs=pl.BlockSpec((tq, HEADS * NOPE), lambda b, t, qi, kj: (b * nq + qi[t], 0)),
        scratch_shapes=[pltpu.VMEM((HEADS, tq, LANES), F32), pltpu.VMEM((HEADS, tq, LANES), F32)],
    )
    return pl.pallas_call(
        functools.partial(_attn_kernel, tq=tq, tk=tk, past=past, kv_len=kv_len, nk=nk),
        grid_spec=grid_spec,
        out_shape=jax.ShapeDtypeStruct((batch * seq, HEADS * NOPE), F32),
        compiler_params=pltpu.CompilerParams(dimension_semantics=("parallel", "arbitrary"),
                                             vmem_limit_bytes=VMEM_LIMIT),
        name="attention",
    )(qi_tab, kj_tab, q, k, v)


def _wkv_kernel(rw_ref, m0_ref, tri_ref, y_ref, mout_ref, m_scr, *, nsteps, nch):
    j = pl.program_id(1)

    @pl.when(j == 0)
    def _load():
        m_scr[...] = m0_ref[0]

    L = CHUNK
    lane = lax.broadcasted_iota(jnp.int32, (L, LANES), 1)
    even = lane < RW_HEAD
    ri = lax.broadcasted_iota(jnp.int32, (LANES, LANES), 0)
    ci = lax.broadcasted_iota(jnp.int32, (LANES, LANES), 1)
    tr = ri & (L - 1)
    tc = ci & (L - 1)
    strict = tr > tc
    incl = tr >= tc
    blk16 = (ri >> 4) == (ci >> 4)
    ident = jnp.where(ri == ci, 1.0, 0.0).astype(F32)

    def stack(t):
        return jnp.concatenate([jnp.where(even, t, 0.0), jnp.where(even, 0.0, t)], axis=0)

    pairs = range(HEAD_PAIRS)
    each = lambda fn, *cols: [fn(*args) for args in zip(*cols)]
    cat0 = lambda *xs: jnp.concatenate(xs, axis=0)
    cat1 = lambda *xs: jnp.concatenate(xs, axis=1)
    sls = [slice(p * LANES, (p + 1) * LANES) for p in pairs]

    a_s, r_s, b_s, k_s, bh_s, kh_s, v_s, gcol = [], [], [], [], [], [], [], []
    tri = tri_ref[...]
    for c in range(nch):
        rows = slice(c * L, (c + 1) * L)
        r = rw_ref[rows, 0:512]
        lw = rw_ref[rows, 512:1024]
        k = rw_ref[rows, 1024:1536]
        v = rw_ref[rows, 1536:2048]
        a = rw_ref[rows, 2048:2560]
        b = rw_ref[rows, 2560:3072]
        lw_hi, lw_rest = lw.astype(BF16), lw - lw.astype(BF16).astype(F32)
        lw_mid = lw_rest.astype(BF16)
        lw_lo = (lw_rest - lw_mid.astype(F32)).astype(BF16)
        cum = _dot(tri, lw_hi) + (_dot(tri, lw_mid) + _dot(tri, lw_lo))
        gt = cum[L - 1:L, :]
        en = jnp.exp(-cum)
        eh = jnp.exp(gt - cum)
        egt = jnp.exp(gt)
        for dst, val in ((a_s, a * jnp.exp(cum - lw)), (r_s, r * jnp.exp(cum)), (b_s, b * en), (k_s, k * en),
                         (bh_s, b * eh), (kh_s, k * eh), (v_s, v)):
            dst.extend(stack(val[:, sl]) for sl in sls)
        gcol.extend(jnp.transpose(jnp.broadcast_to(egt[:, sl], (LANES, LANES))) for sl in sls)
    prod = each(lambda a_, r_, b_, k_: _mx(cat0(a_, r_), cat0(b_, k_), P_PROD, _NT), a_s, r_s, b_s, k_s)
    n_ab = [jnp.where(strict, x[0:LANES, 0:LANES], 0.0) for x in prod]
    a_ak = [jnp.where(strict, x[0:LANES, LANES:2 * LANES], 0.0) for x in prod]
    a_rb = [jnp.where(incl, x[LANES:2 * LANES, 0:LANES], 0.0) for x in prod]
    a_rk = [jnp.where(incl, x[LANES:2 * LANES, LANES:2 * LANES], 0.0) for x in prod]
    inv = lambda x, y_: _mx(x, y_, P_INV)
    d1 = [jnp.where(blk16, x, 0.0) for x in n_ab]
    e1 = each(lambda x, d: x - d, n_ab, d1)
    p1 = [ident + d for d in d1]
    d2 = each(inv, d1, d1)
    p2 = each(lambda p_, d: p_ + inv(p_, d), p1, d2)
    d4 = each(inv, d2, d2)
    p3 = each(lambda p_, d: p_ + inv(p_, d), p2, d4)
    d8 = each(inv, d4, d4)
    t_d = each(lambda p_, d: p_ + inv(p_, d), p3, d8)
    f1 = each(inv, t_d, e1)
    f2 = each(inv, f1, f1)
    if1 = [ident + f for f in f1]
    g1 = each(lambda i_, f: i_ + inv(i_, f), if1, f2)
    t_m = each(inv, g1, t_d)

    m = [m_scr[p] for p in pairs]
    for c in range(nch):
        ch = slice(c * HEAD_PAIRS, (c + 1) * HEAD_PAIRS)
        w = each(lambda a_, ak, m_, v_: _mx(cat1(a_, ak), cat0(m_, v_), P_W), a_s[ch], a_ak[ch], m, v_s[ch])
        u = each(lambda t_, w_: _mx(t_, w_, P_U), t_m[ch], w)
        y = each(lambda r_, rb, rk, m_, u_, v_: _mx(cat1(r_, rb, rk), cat0(m_, u_, v_), P_Y),
                 r_s[ch], a_rb[ch], a_rk[ch], m, u, v_s[ch])
        m = each(lambda b_, k_, u_, v_, m_, g_: m_ * g_ + _mx(cat0(b_, k_), cat0(u_, v_), P_M, _TN),
                 bh_s[ch], kh_s[ch], u, v_s[ch], m, gcol[ch])
        for p in pairs:
            y_ref[c * L:(c + 1) * L, sls[p]] = y[p][0:L] + y[p][L:2 * L]
    for p in pairs:
        m_scr[p] = m[p]

    @pl.when(j == nsteps - 1)
    def _store():
        mout_ref[0] = m_scr[...]


def _wkv_stage(rw, m0, tri, *, batch, seq):
    assert seq % CHUNK == 0
    nch = max(d for d in (4, 2, 1) if (seq // CHUNK) % d == 0)
    rows = nch * CHUNK
    nsteps = seq // rows
    return pl.pallas_call(
        functools.partial(_wkv_kernel, nsteps=nsteps, nch=nch),
        grid=(batch, nsteps),
        in_specs=[pl.BlockSpec((rows, 6 * RW_WIDTH), lambda b, j: (b * nsteps + j, 0)),
                  pl.BlockSpec((1, HEAD_PAIRS, LANES, LANES), lambda b, j: (b, 0, 0, 0)),
                  _const_spec((CHUNK, CHUNK))],
        out_specs=(pl.BlockSpec((rows, RW_WIDTH), lambda b, j: (b * nsteps + j, 0)),
                   pl.BlockSpec((1, HEAD_PAIRS, LANES, LANES), lambda b, j: (b, 0, 0, 0))),
        out_shape=(jax.ShapeDtypeStruct((batch * seq, RW_WIDTH), F32),
                   jax.ShapeDtypeStruct((batch, HEAD_PAIRS, LANES, LANES), F32)),
        scratch_shapes=[pltpu.VMEM((HEAD_PAIRS, LANES, LANES), F32)],
        compiler_params=pltpu.CompilerParams(dimension_semantics=("parallel", "arbitrary"),
                                             vmem_limit_bytes=VMEM_LIMIT),
        name="wkv",
    )(rw, m0, tri)


def _out_kernel(x_ref, oa_ref, y_ref, bon_ref, ga_ref, gb_ref, ma_ref, mb_ref, v512_ref, seg_ref,
                woa_ref, wob_ref, wo_ref, out_ref):
    v512 = v512_ref[...]
    lnx_w, lnx_b = v512[5:6], v512[6:7]
    seg = seg_ref[...]
    oa = (oa_ref[...] * ga_ref[...]).astype(BF16)
    y = y_ref[...]
    d = y - _seg_sum(y, seg) * (1.0 / RW_HEAD)
    var = _seg_sum(d * d, seg) * (1.0 / RW_HEAD)
    yn = d * lax.rsqrt(var + GN_EPS) * lnx_w + lnx_b
    ob = ((yn + bon_ref[...]) * gb_ref[...]).astype(BF16)
    m = ma_ref[...] * _dot(oa, woa_ref[...]) + mb_ref[...] * _dot(ob, wob_ref[...])
    out_ref[...] = x_ref[...] + _dot(m.astype(BF16), wo_ref[...])


def _out_stage(x2d, oa, y, bon, ga, gb, ma, mb, lw):
    n = x2d.shape[0]
    tm = min(256, n)
    assert n % tm == 0
    row = lambda w: pl.BlockSpec((tm, w), lambda i: (i, 0))
    return pl.pallas_call(
        _out_kernel,
        grid=(n // tm,),
        in_specs=[row(D_MODEL), row(512), row(512), row(512), row(512), row(512), row(D_MODEL), row(D_MODEL),
                  _const_spec((8, RW_WIDTH)), _const_spec((RW_WIDTH, RW_WIDTH)),
                  _const_spec((512, D_MODEL)), _const_spec((512, D_MODEL)), _const_spec((D_MODEL, D_MODEL))],
        out_specs=row(D_MODEL),
        out_shape=jax.ShapeDtypeStruct((n, D_MODEL), F32),
        compiler_params=pltpu.CompilerParams(dimension_semantics=("parallel",), vmem_limit_bytes=VMEM_LIMIT),
        name="out_stage",
    )(x2d, oa, y, bon, ga, gb, ma, mb, lw["v512"], lw["seg"], lw["w_out_a"], lw["w_out_b"], lw["w_o"])


def _swap_halves(w):
    half = ROPE // 2
    return jnp.concatenate([w[..., half:], w[..., :half]], axis=-1)


def _prep_weights(norm_w, w_in, q_norm_w, kv_norm_w, w_uq, w_ukv, qn_nope, qn_rope, kn_nope, kn_rope,
                  mu_shift, w0, w2, a0, a2, k_k, k_a, r_k, lnx_w, lnx_b, w_out_a, w_out_b, w_o):
    depth = w_in.shape[0]
    z = lambda *s: jnp.zeros((depth,) + s, F32)
    kpe = w_in[..., 384:416]
    w_in_p = jnp.concatenate([w_in[..., :384], kpe, z(D_MODEL, 96), _swap_halves(kpe), z(D_MODEL, 96),
                              w_in[..., 416:]], axis=-1).astype(BF16)
    uq = w_uq.reshape(depth, Q_RANK, HEADS, QK)
    uq_p = jnp.concatenate([uq, z(Q_RANK, HEADS, 32)], axis=-1).reshape(depth, Q_RANK, HEADS * LANES)
    uq_s = jnp.concatenate([z(Q_RANK, HEADS, NOPE), _swap_halves(uq[..., NOPE:]), z(Q_RANK, HEADS, 32)],
                           axis=-1).reshape(depth, Q_RANK, HEADS * LANES)
    ukv = w_ukv.reshape(depth, KV_RANK, HEADS, 2 * NOPE)
    zk = z(KV_RANK, HEADS, NOPE)
    k_part = jnp.concatenate([ukv[..., :NOPE], zk], axis=-1)
    vv = ukv[..., NOPE:]
    odd = (jnp.arange(HEADS) % 2 == 1)[None, None, :, None]
    v_part = jnp.where(odd, jnp.concatenate([zk, vv], axis=-1), jnp.concatenate([vv, zk], axis=-1))
    ukv_p = jnp.concatenate([k_part.reshape(depth, KV_RANK, HEADS * LANES),
                             v_part.reshape(depth, KV_RANK, HEADS * LANES)], axis=-1).astype(BF16)
    zg = lambda n: jnp.zeros((depth, n), F32)
    g128 = jnp.stack([
        jnp.concatenate([qn_nope, qn_rope, zg(32)], axis=-1),
        jnp.concatenate([zg(NOPE), _swap_halves(qn_rope), zg(32)], axis=-1),
        jnp.concatenate([kn_rope, zg(96)], axis=-1),
        jnp.concatenate([_swap_halves(kn_rope), zg(96)], axis=-1),
        kv_norm_w,
        jnp.concatenate([kn_nope, zg(NOPE)], axis=-1),
        zg(LANES), zg(LANES)], axis=1)
    v512 = jnp.stack([w0, a0, k_k, k_a, r_k.reshape(depth, RW_WIDTH), lnx_w, lnx_b, zg(RW_WIDTH)], axis=1)
    seg_id = jnp.arange(RW_WIDTH) // RW_HEAD
    seg = (seg_id[:, None] == seg_id[None, :]).astype(BF16)
    onecol = jnp.stack([jnp.arange(LANES) == NOPE, jnp.arange(LANES) == 0]).astype(F32)
    return dict(
        norm_w=norm_w[:, None, :], w_in=w_in_p, q_norm_w=q_norm_w[:, None, :],
        w_uq=uq_p.astype(BF16), w_uq_sw=uq_s.astype(BF16), w_ukv=ukv_p, g128=g128, mu=mu_shift[:, None, :],
        v512=v512, w2=jnp.concatenate([w2, z(LORA, RW_WIDTH)], axis=1),
        a2=jnp.concatenate([z(LORA, RW_WIDTH), a2], axis=1),
        w_out_a=w_out_a.astype(BF16), w_out_b=w_out_b.astype(BF16), w_o=w_o.astype(BF16),
    ), dict(seg=seg, onecol=onecol)


def _rope_table(past, seq, batch):
    half = ROPE // 2
    inv = ROPE_THETA ** (-jnp.arange(half, dtype=F32) / half)
    pos = past + jnp.arange(seq, dtype=jnp.int32)
    ang = pos.astype(F32)[:, None] * inv[None, :]
    cos, sin = jnp.cos(ang), jnp.sin(ang)
    one, zero = jnp.ones((seq, NOPE), F32), jnp.zeros((seq, NOPE), F32)
    z32, z96 = jnp.zeros((seq, 32), F32), jnp.zeros((seq, 96), F32)
    tab = jnp.concatenate([one, cos, cos, z32, zero, -sin, sin, z32, cos, cos, z96, -sin, sin, z96], axis=-1)
    return jnp.tile(tab, (batch, 1))


def _state_to_pairs(s):
    b = s.shape[0]
    m = jnp.swapaxes(s.astype(F32), -1, -2).reshape(b, HEAD_PAIRS, 2, RW_HEAD, RW_HEAD)
    zero = jnp.zeros_like(m[:, :, 0])
    top = jnp.concatenate([m[:, :, 0], zero], axis=-1)
    bot = jnp.concatenate([zero, m[:, :, 1]], axis=-1)
    return jnp.concatenate([top, bot], axis=-2)


def _pairs_to_state(m):
    b = m.shape[0]
    e = m[:, :, :RW_HEAD, :RW_HEAD]
    o = m[:, :, RW_HEAD:, RW_HEAD:]
    return jnp.swapaxes(jnp.stack([e, o], axis=2).reshape(b, HEADS, RW_HEAD, RW_HEAD), -1, -2)


def _round_up(n, m):
    return (n + m - 1) // m * m


def _group_layer(x2d, tab, lw, ckv_past, kpe_past, m0, sh0, tri, *, batch, seq, past):
    n = batch * seq
    q, ckv, kpe, rw, bon, ga, gb, ma, mb, shift = _in_stage(x2d, tab, sh0, lw, seq=seq)
    ckv3 = ckv.reshape(batch, seq, KV_RANK)
    kpe3 = kpe.reshape(batch, seq, ROPE)
    kv_len = past + seq
    if past > 0:
        ckv_all = jnp.concatenate([ckv_past, ckv3], axis=1)
        kpe_all = jnp.concatenate([kpe_past, kpe3], axis=1)
    else:
        ckv_all, kpe_all = ckv3, kpe3
    if seq >= 512:
        tq, tk = 512, 512
        kv_pad = _round_up(kv_len, tk)
    else:
        tq = seq
        kv_pad = _round_up(kv_len, LANES)
        tk = kv_pad
    ckv_all = jnp.pad(ckv_all, ((0, 0), (0, kv_pad - kv_len), (0, 0)))
    kpe_all = jnp.pad(kpe_all, ((0, 0), (0, kv_pad - kv_len), (NOPE, LANES - QK)))
    kv_tile = max(t for t in (128, 256, 384, 512, 768, 1024, 1408) if kv_pad % t == 0)
    k_att, v_att = _kv_stage(ckv_all.reshape(batch * kv_pad, KV_RANK), kpe_all.reshape(batch * kv_pad, LANES),
                             lw, tile=kv_tile)
    oa = _attn_stage(q, k_att, v_att, batch=batch, seq=seq, past=past, kv_len=kv_len, kv_pad=kv_pad, tq=tq, tk=tk)
    y, m1 = _wkv_stage(rw, m0, tri, batch=batch, seq=seq)
    out = _out_stage(x2d, oa, y, bon, ga, gb, ma, mb, lw)
    return out, ckv3, kpe3, _pairs_to_state(m1), shift.reshape(batch, 1, SHIFT_WIDTH)


def kernel(x_prompt, x_sample, cache_ckv, cache_kpe, state_wkv, state_shift, norm_w, w_in, q_norm_w, kv_norm_w,
           w_uq, w_ukv, qn_nope, qn_rope, kn_nope, kn_rope, mu_shift, w0, w2, a0, a2, k_k, k_a, r_k, lnx_w, lnx_b,
           w_out_a, w_out_b, w_o):
    depth = w_in.shape[0]
    bp, tp, _ = x_prompt.shape
    bs, ts, _ = x_sample.shape
    past = cache_ckv.shape[2]
    assert past % CHUNK == 0 and tp % CHUNK == 0 and ts % CHUNK == 0
    stacked, shared = _prep_weights(norm_w, w_in, q_norm_w, kv_norm_w, w_uq, w_ukv, qn_nope, qn_rope, kn_nope,
                                    kn_rope, mu_shift, w0, w2, a0, a2, k_k, k_a, r_k, lnx_w, lnx_b,
                                    w_out_a, w_out_b, w_o)
    tab_p = _rope_table(0, tp, bp)
    tab_s = _rope_table(past, ts, bs)
    tri = (jnp.arange(CHUNK)[:, None] >= jnp.arange(CHUNK)[None, :]).astype(BF16)
    zero_m = jnp.zeros((bp, HEAD_PAIRS, LANES, LANES), F32)
    zero_sh = jnp.zeros((bp, 1, SHIFT_WIDTH), F32)

    yp = x_prompt.reshape(bp * tp, D_MODEL)
    ys = x_sample.reshape(bs * ts, D_MODEL)
    outs_p, outs_s = [], []
    for i in range(depth):
        lw = {name: val[i] for name, val in stacked.items()}
        lw.update(shared)
        yp, *rest_p = _group_layer(yp, tab_p, lw, None, None, zero_m, zero_sh, tri, batch=bp, seq=tp, past=0)
        outs_p.append(rest_p)
        ys, *rest_s = _group_layer(ys, tab_s, lw, cache_ckv[i], cache_kpe[i], _state_to_pairs(state_wkv[i]),
                                   state_shift[i], tri, batch=bs, seq=ts, past=past)
        outs_s.append(rest_s)
    stack = lambda outs, j: jnp.stack([o[j] for o in outs])
    return (yp.reshape(bp, tp, D_MODEL), ys.reshape(bs, ts, D_MODEL),
            stack(outs_p, 0), stack(outs_p, 1), stack(outs_p, 2), stack(outs_p, 3),
            stack(outs_s, 0), stack(outs_s, 1), stack(outs_s, 2), stack(outs_s, 3))
```

```python
import functools

import jax
import jax.numpy as jnp
from jax import lax
from jax.experimental import pallas as pl
from jax.experimental.pallas import tpu as pltpu

F32 = jnp.float32
BF16 = jnp.bfloat16
HI = lax.Precision.HIGHEST

D_MODEL = 1024
CHUNK = 64
CHUNK_SHIFT = 6
EPS = 1e-6
GN_EPS = 64e-5
HEADS = 8
NOPE = 64
ROPE = 32
QK = NOPE + ROPE
Q_RANK = 256
KV_RANK = 128
ROPE_THETA = 10000.0
RW_WIDTH = 512
RW_HEAD = 64
LORA = 64
SHIFT_WIDTH = 3 * RW_WIDTH + 2 * LORA
LANES = 128
HEAD_PAIRS = HEADS // 2
NEG = -1e30
P_PROD, P_INV, P_W, P_U, P_Y, P_M = 1, 1, 1, 1, 1, 1
LOG2_E = 1.4426950408889634

W_LAT = 640
C_GA = 0
C_ZS = C_GA + 512
C_GB = C_ZS + SHIFT_WIDTH
C_MA = C_GB + 512
C_MB = C_MA + D_MODEL
C_END = C_MB + D_MODEL

VMEM_LIMIT = 56 * 1024 * 1024


def _const_spec(shape):
    n = len(shape)
    return pl.BlockSpec(shape, lambda *_: (0,) * n)


_NN = (((1,), (0,)), ((), ()))
_NT = (((1,), (1,)), ((), ()))
_TN = (((0,), (0,)), ((), ()))


def _dot(a, b):
    return jnp.dot(a, b, preferred_element_type=F32)


def _dot_nt(a, b):
    return lax.dot_general(a, b, _NT, preferred_element_type=F32)


def _split(x):
    hi = x.astype(BF16)
    return hi, (x - hi.astype(F32)).astype(BF16)


def _mx(a, b, passes, dims=_NN):
    if passes == 6:
        return lax.dot_general(a, b, dims, precision=HI, preferred_element_type=F32)
    dg = lambda x, y: lax.dot_general(x, y, dims, preferred_element_type=F32)
    if passes == 1:
        return dg(a.astype(BF16), b.astype(BF16))
    ah, al = _split(a)
    bh, bl = _split(b)
    return dg(ah, bh) + (dg(ah, bl) + dg(al, bh))


def _seg_sum(x, seg, passes=2):
    hi = x.astype(BF16)
    if passes == 1:
        return _dot(hi, seg)
    lo = (x - hi.astype(F32)).astype(BF16)
    return _dot(hi, seg) + _dot(lo, seg)


def _in_kernel(x_ref, tab_ref, sh0_ref, normw_ref, wlat_ref, win_ref, qnw_ref, wuq_ref, wuqs_ref, g128_ref, mu_ref,
               v512_ref, w2_ref, a2_ref, seg_ref,
               q_ref, ckv_ref, kpe_ref, rw_ref, bon_ref, ga_ref, gb_ref, ma_ref, mb_ref, shift_ref,
               carry_ref, *, tm, seq, scale):
    i = pl.program_id(0)
    x = x_ref[...]
    h = x * lax.rsqrt(jnp.mean(x * x, axis=-1, keepdims=True) + EPS) * normw_ref[...]
    hb = h.astype(BF16)
    tab = tab_ref[...]
    g128 = g128_ref[...]

    z1 = _dot(hb, wlat_ref[...])
    cq = z1[:, :Q_RANK]
    cqn = cq * lax.rsqrt(jnp.mean(cq * cq, axis=-1, keepdims=True) + EPS) * qnw_ref[...]
    kvl = z1[:, Q_RANK:Q_RANK + KV_RANK]
    ckv_ref[...] = kvl * lax.rsqrt(jnp.mean(kvl * kvl, axis=-1, keepdims=True) + EPS) * g128[4:5]
    t1 = z1[:, 384:512]
    t2 = z1[:, 512:640]
    kinv = lax.rsqrt(jnp.sum(t1 * t1, axis=-1, keepdims=True) * (1.0 / ROPE) + EPS)
    kr = (t1 * g128[2:3] * tab[:, 256:384] + t2 * g128[3:4] * tab[:, 384:512]) * kinv
    kpe_ref[...] = kr[:, :ROPE]

    cqb = cqn.astype(BF16)
    qr = _dot(cqb, wuq_ref[...])
    qs = _dot(cqb, wuqs_ref[...])
    lane = lax.broadcasted_iota(jnp.int32, (1, LANES), 1)
    is_nope = lane < NOPE
    cos_f = tab[:, 0:128]
    sin_f = tab[:, 128:256]
    for hd in range(HEADS):
        t = qr[:, hd * LANES:(hd + 1) * LANES]
        sq = t * t
        ms_n = jnp.sum(jnp.where(is_nope, sq, 0.0), axis=-1, keepdims=True) * (1.0 / NOPE)
        ms_r = jnp.sum(jnp.where(is_nope, 0.0, sq), axis=-1, keepdims=True) * (1.0 / ROPE)
        inv = jnp.where(is_nope, lax.rsqrt(ms_n + EPS), lax.rsqrt(ms_r + EPS))
        qh = (t * g128[0:1] * cos_f + qs[:, hd * LANES:(hd + 1) * LANES] * g128[1:2] * sin_f) * (inv * scale)
        q_ref[hd] = qh.astype(BF16)

    ga_ref[...] = jax.nn.silu(_dot(hb, win_ref[:, C_GA:C_ZS])).astype(BF16)
    gb_ref[...] = jax.nn.silu(_dot(hb, win_ref[:, C_GB:C_MA])).astype(BF16)
    ma_ref[...] = jax.nn.sigmoid(_dot(hb, win_ref[:, C_MA:C_MB])).astype(BF16)
    mb_ref[...] = jax.nn.sigmoid(_dot(hb, win_ref[:, C_MB:C_END])).astype(BF16)

    zs = _dot(hb, win_ref[:, C_ZS:C_GB])
    rows = lax.broadcasted_iota(jnp.int32, (tm, 1), 0)
    prev = pltpu.roll(zs, 1, axis=0)
    if seq >= tm:
        tiles_per_seq = seq // tm

        @pl.when(i == 0)
        def _init_carry():
            carry_ref[...] = jnp.zeros(carry_ref.shape, F32)

        first = jnp.where(i % tiles_per_seq == 0, sh0_ref[0], carry_ref[...])
        prev = jnp.where(rows == 0, first, prev)
        carry_ref[...] = zs[tm - 1:tm, :]
        shift_ref[0] = zs[tm - 1:tm, :]
    else:
        for s in range(tm // seq):
            prev = jnp.where(rows == s * seq, sh0_ref[0, s:s + 1, :], prev)
            shift_ref[0, s:s + 1, :] = zs[(s + 1) * seq - 1:(s + 1) * seq, :]
    zm = zs + (prev - zs) * mu_ref[...]
    r = zm[:, 0:512]
    k = zm[:, 512:1024]
    v = zm[:, 1024:1536]
    t13 = zm[:, 1536:1664]
    v512 = v512_ref[...]
    w0, a0, k_k, k_a, r_k = v512[0:1], v512[1:2], v512[2:3], v512[3:4], v512[4:5]
    seg = seg_ref[...]
    log_w = -jax.nn.softplus(-(w0 + _mx(jnp.tanh(t13), w2_ref[...], 3))) - 0.5
    lw = -jnp.exp(log_w)
    aic = jax.nn.sigmoid(a0 + _mx(t13, a2_ref[...], 3))
    kkr = k * k_k
    kk = kkr / jnp.maximum(jnp.sqrt(_seg_sum(kkr * kkr, seg, 1)), 1e-12)
    k2 = k * (1.0 + (aic - 1.0) * k_a)
    bon_ref[...] = (_seg_sum(r * k2 * r_k, seg, 1) * v).astype(BF16)
    rw_ref[:, 0:512] = r
    rw_ref[:, 512:1024] = lw
    rw_ref[:, 1024:1536] = k2
    rw_ref[:, 1536:2048] = v
    rw_ref[:, 2048:2560] = -kk
    rw_ref[:, 2560:3072] = kk * aic


def _in_stage(x2d, tab, sh0, lw, *, seq):
    n = x2d.shape[0]
    tm = 256 if seq >= 256 else min(256, n)
    assert n % tm == 0 and (seq % tm == 0 or tm % seq == 0)
    nseq = max(1, tm // seq)
    batch = n // seq
    sh0 = sh0.reshape(batch // nseq, nseq, SHIFT_WIDTH)
    if seq >= tm:
        sh_map = lambda i: (i // (seq // tm), 0, 0)
    else:
        sh_map = lambda i: (i, 0, 0)
    row = lambda w: pl.BlockSpec((tm, w), lambda i: (i, 0))
    out_shape = (
        jax.ShapeDtypeStruct((HEADS, n, LANES), BF16),
        jax.ShapeDtypeStruct((n, KV_RANK), F32),
        jax.ShapeDtypeStruct((n, ROPE), F32),
        jax.ShapeDtypeStruct((n, 6 * RW_WIDTH), F32),
        jax.ShapeDtypeStruct((n, RW_WIDTH), BF16),
        jax.ShapeDtypeStruct((n, 512), BF16),
        jax.ShapeDtypeStruct((n, 512), BF16),
        jax.ShapeDtypeStruct((n, D_MODEL), BF16),
        jax.ShapeDtypeStruct((n, D_MODEL), BF16),
        jax.ShapeDtypeStruct((batch // nseq, nseq, SHIFT_WIDTH), F32),
    )
    out_specs = (
        pl.BlockSpec((HEADS, tm, LANES), lambda i: (0, i, 0)),
        row(KV_RANK), row(ROPE), row(6 * RW_WIDTH), row(RW_WIDTH), row(512), row(512), row(D_MODEL), row(D_MODEL),
        pl.BlockSpec((1, nseq, SHIFT_WIDTH), sh_map),
    )
    in_specs = [
        row(D_MODEL), row(512), pl.BlockSpec((1, nseq, SHIFT_WIDTH), sh_map),
        _const_spec((1, D_MODEL)), _const_spec((D_MODEL, W_LAT)),
        pl.BlockSpec((D_MODEL, C_END), lambda i: (0, 0), pipeline_mode=pl.Buffered(1)),
        _const_spec((1, Q_RANK)), _const_spec((Q_RANK, HEADS * LANES)), _const_spec((Q_RANK, HEADS * LANES)),
        _const_spec((8, LANES)), _const_spec((1, SHIFT_WIDTH)), _const_spec((8, RW_WIDTH)),
        _const_spec((LANES, RW_WIDTH)), _const_spec((LANES, RW_WIDTH)), _const_spec((RW_WIDTH, RW_WIDTH)),
    ]
    return pl.pallas_call(
        functools.partial(_in_kernel, tm=tm, seq=seq, scale=QK ** -0.5 * LOG2_E),
        grid=(n // tm,),
        in_specs=in_specs,
        out_specs=out_specs,
        out_shape=out_shape,
        scratch_shapes=[pltpu.VMEM((1, SHIFT_WIDTH), F32)],
        compiler_params=pltpu.CompilerParams(dimension_semantics=("arbitrary",), vmem_limit_bytes=VMEM_LIMIT),
        name="in_stage",
    )(x2d, tab, sh0, lw["norm_w"], lw["w_lat"], lw["w_in"], lw["q_norm_w"], lw["w_uq"], lw["w_uq_sw"], lw["g128"], lw["mu"],
      lw["v512"], lw["w2"], lw["a2"], lw["seg"])


def _kv_kernel(ckv_ref, kpe_ref, wukv_ref, g128_ref, one_ref, k_ref, v_ref):
    kv = _dot(ckv_ref[...].astype(BF16), wukv_ref[...])
    kpe = kpe_ref[...]
    gkn = g128_ref[5:6]
    for hd in range(HEADS):
        kr = kv[:, hd * LANES:(hd + 1) * LANES]
        inv = lax.rsqrt(jnp.sum(kr * kr, axis=-1, keepdims=True) * (1.0 / NOPE) + EPS)
        k_ref[hd] = (kr * inv * gkn + kpe).astype(BF16)
        vr = kv[:, (HEADS + hd) * LANES:(HEADS + hd + 1) * LANES]
        v_ref[hd] = (vr + one_ref[hd % 2:hd % 2 + 1]).astype(BF16)


def _kv_stage(ckv2d, kpe2d, lw, *, tile):
    n = ckv2d.shape[0]
    assert n % tile == 0
    row = pl.BlockSpec((tile, LANES), lambda i: (i, 0))
    hspec = pl.BlockSpec((HEADS, tile, LANES), lambda i: (0, i, 0))
    return pl.pallas_call(
        _kv_kernel,
        grid=(n // tile,),
        in_specs=[row, row, _const_spec((KV_RANK, 2 * HEADS * LANES)), _const_spec((8, LANES)),
                  _const_spec((2, LANES))],
        out_specs=(hspec, hspec),
        out_shape=(jax.ShapeDtypeStruct((HEADS, n, LANES), BF16),) * 2,
        compiler_params=pltpu.CompilerParams(dimension_semantics=("parallel",), vmem_limit_bytes=VMEM_LIMIT),
        name="kv_stage",
    )(ckv2d, kpe2d, lw["w_ukv"], lw["g128"], lw["onecol"])


def _last_key_tile(qi, *, tq, tk, past):
    return (past + qi * tq + tq - 1) // tk


def _attn_kernel(qi_ref, kj_ref, q_ref, k_ref, v_ref, o_ref, m_scr, acc_scr, *, tq, tk, past, kv_len, nk):
    t = pl.program_id(1)
    qi = qi_ref[t]
    kj = kj_ref[t]

    @pl.when(kj == 0)
    def _init():
        m_scr[...] = jnp.full(m_scr.shape, NEG, F32)
        acc_scr[...] = jnp.zeros(acc_scr.shape, F32)

    q_first = past + qi * tq
    k_first = kj * tk
    k_last = k_first + tk - 1
    unmasked = jnp.logical_and((k_last >> CHUNK_SHIFT) <= (q_first >> CHUNK_SHIFT), k_last < kv_len)

    def compute(masked):
        if masked:
            rows = q_first + lax.broadcasted_iota(jnp.int32, (tq, tk), 0)
            cols = k_first + lax.broadcasted_iota(jnp.int32, (tq, tk), 1)
            ok = jnp.logical_and((cols >> CHUNK_SHIFT) <= (rows >> CHUNK_SHIFT), cols < kv_len)
        s_next = _dot_nt(q_ref[0], k_ref[0])
        for hd in range(HEADS):
            s = s_next
            if hd + 1 < HEADS:
                s_next = _dot_nt(q_ref[hd + 1], k_ref[hd + 1])
            if masked:
                s = jnp.where(ok, s, NEG)
            m_prev = m_scr[hd]
            m_new = jnp.maximum(m_prev, jnp.max(s, axis=-1, keepdims=True))
            alpha = jnp.exp2(m_prev - m_new)
            p = jnp.exp2((s - jnp.concatenate([m_new] * (tk // LANES), axis=1)).astype(BF16))
            acc_scr[hd] = acc_scr[hd] * alpha + _dot(p, v_ref[hd])
            m_scr[hd] = m_new

    @pl.when(unmasked)
    def _full():
        compute(False)

    @pl.when(jnp.logical_not(unmasked))
    def _diag():
        compute(True)

    @pl.when(kj == jnp.minimum(_last_key_tile(qi, tq=tq, tk=tk, past=past), nk - 1))
    def _finish():
        lane = lax.broadcasted_iota(jnp.int32, (1, LANES), 1)
        for p in range(HEAD_PAIRS):
            ae = acc_scr[2 * p]
            ao = acc_scr[2 * p + 1]
            o_ref[:, p * LANES:(p + 1) * LANES] = jnp.where(lane < 64, ae / ae[:, 64:65], ao / ao[:, 0:1])


def _attn_stage(q, k, v, *, batch, seq, past, kv_len, kv_pad, tq, tk):
    nq = seq // tq
    nk = kv_pad // tk
    assert seq % tq == 0 and kv_pad % tk == 0 and tq % CHUNK == 0 and past % CHUNK == 0
    steps = [(qi, kj) for qi in range(nq)
             for kj in range(min(_last_key_tile(qi, tq=tq, tk=tk, past=past), nk - 1) + 1)]
    qi_tab = jnp.asarray([s[0] for s in steps], jnp.int32)
    kj_tab = jnp.asarray([s[1] for s in steps], jnp.int32)
    kv_map = lambda b, t, qi, kj: (0, b * nk + kj[t], 0)
    grid_spec = pltpu.PrefetchScalarGridSpec(
        num_scalar_prefetch=2,
        grid=(batch, len(steps)),
        in_specs=[pl.BlockSpec((HEADS, tq, LANES), lambda b, t, qi, kj: (0, b * nq + qi[t], 0)),
                  pl.BlockSpec((HEADS, tk, LANES), kv_map),
                  pl.BlockSpec((HEADS, tk, LANES), kv_map)],
        out_specs=pl.BlockSpec((tq, HEADS * NOPE), lambda b, t, qi, kj: (b * nq + qi[t], 0)),
        scratch_shapes=[pltpu.VMEM((HEADS, tq, LANES), F32), pltpu.VMEM((HEADS, tq, LANES), F32)],
    )
    return pl.pallas_call(
        functools.partial(_attn_kernel, tq=tq, tk=tk, past=past, kv_len=kv_len, nk=nk),
        grid_spec=grid_spec,
        out_shape=jax.ShapeDtypeStruct((batch * seq, HEADS * NOPE), F32),
        compiler_params=pltpu.CompilerParams(dimension_semantics=("parallel", "arbitrary"),
                                             vmem_limit_bytes=VMEM_LIMIT),
        name="attention",
    )(qi_tab, kj_tab, q, k, v)


def _wkv_kernel(rw_ref, m0_ref, tri_ref, y_ref, mout_ref, m_scr, *, nsteps, nch):
    j = pl.program_id(1)

    @pl.when(j == 0)
    def _load():
        m_scr[...] = m0_ref[0]

    L = CHUNK
    lane = lax.broadcasted_iota(jnp.int32, (L, LANES), 1)
    even = lane < RW_HEAD
    ri = lax.broadcasted_iota(jnp.int32, (LANES, LANES), 0)
    ci = lax.broadcasted_iota(jnp.int32, (LANES, LANES), 1)
    tr = ri & (L - 1)
    tc = ci & (L - 1)
    strict = tr > tc
    incl = tr >= tc
    blk16 = (ri >> 4) == (ci >> 4)
    ident = jnp.where(ri == ci, 1.0, 0.0).astype(F32)

    def stack(t):
        return jnp.concatenate([jnp.where(even, t, 0.0), jnp.where(even, 0.0, t)], axis=0)

    pairs = range(HEAD_PAIRS)
    each = lambda fn, *cols: [fn(*args) for args in zip(*cols)]
    cat0 = lambda *xs: jnp.concatenate(xs, axis=0)
    cat1 = lambda *xs: jnp.concatenate(xs, axis=1)
    sls = [slice(p * LANES, (p + 1) * LANES) for p in pairs]

    a_s, r_s, b_s, k_s, bh_s, kh_s, v_s, gcol = [], [], [], [], [], [], [], []
    tri = tri_ref[...]
    for c in range(nch):
        rows = slice(c * L, (c + 1) * L)
        r = rw_ref[rows, 0:512]
        lw = rw_ref[rows, 512:1024]
        k = rw_ref[rows, 1024:1536]
        v = rw_ref[rows, 1536:2048]
        a = rw_ref[rows, 2048:2560]
        b = rw_ref[rows, 2560:3072]
        lw_hi, lw_rest = lw.astype(BF16), lw - lw.astype(BF16).astype(F32)
        lw_mid = lw_rest.astype(BF16)
        lw_lo = (lw_rest - lw_mid.astype(F32)).astype(BF16)
        cum = _dot(tri, lw_hi) + (_dot(tri, lw_mid) + _dot(tri, lw_lo))
        gt = cum[L - 1:L, :]
        en = jnp.exp(-cum)
        eh = jnp.exp(gt - cum)
        egt = jnp.exp(gt)
        for dst, val in ((a_s, a * jnp.exp(cum - lw)), (r_s, r * jnp.exp(cum)), (b_s, b * en), (k_s, k * en),
                         (bh_s, b * eh), (kh_s, k * eh), (v_s, v)):
            dst.extend(stack(val[:, sl]) for sl in sls)
        gcol.extend(jnp.transpose(jnp.broadcast_to(egt[:, sl], (LANES, LANES))) for sl in sls)
    prod = each(lambda a_, r_, b_, k_: _mx(cat0(a_, r_), cat0(b_, k_), P_PROD, _NT), a_s, r_s, b_s, k_s)
    n_ab = [jnp.where(strict, x[0:LANES, 0:LANES], 0.0) for x in prod]
    a_ak = [jnp.where(strict, x[0:LANES, LANES:2 * LANES], 0.0) for x in prod]
    a_rb = [jnp.where(incl, x[LANES:2 * LANES, 0:LANES], 0.0) for x in prod]
    a_rk = [jnp.where(incl, x[LANES:2 * LANES, LANES:2 * LANES], 0.0) for x in prod]
    inv = lambda x, y_: _mx(x, y_, P_INV)
    d1 = [jnp.where(blk16, x, 0.0) for x in n_ab]
    e1 = each(lambda x, d: x - d, n_ab, d1)
    p1 = [ident + d for d in d1]
    d2 = each(inv, d1, d1)
    p2 = each(lambda p_, d: p_ + inv(p_, d), p1, d2)
    d4 = each(inv, d2, d2)
    p3 = each(lambda p_, d: p_ + inv(p_, d), p2, d4)
    d8 = each(inv, d4, d4)
    t_d = each(lambda p_, d: p_ + inv(p_, d), p3, d8)
    f1 = each(inv, t_d, e1)
    f2 = each(inv, f1, f1)
    if1 = [ident + f for f in f1]
    g1 = each(lambda i_, f: i_ + inv(i_, f), if1, f2)
    t_m = each(inv, g1, t_d)

    m = [m_scr[p] for p in pairs]
    for c in range(nch):
        ch = slice(c * HEAD_PAIRS, (c + 1) * HEAD_PAIRS)
        w = each(lambda a_, ak, m_, v_: _mx(cat1(a_, ak), cat0(m_, v_), P_W), a_s[ch], a_ak[ch], m, v_s[ch])
        u = each(lambda t_, w_: _mx(t_, w_, P_U), t_m[ch], w)
        y = each(lambda r_, rb, rk, m_, u_, v_: _mx(cat1(r_, rb, rk), cat0(m_, u_, v_), P_Y),
                 r_s[ch], a_rb[ch], a_rk[ch], m, u, v_s[ch])
        m = each(lambda b_, k_, u_, v_, m_, g_: m_ * g_ + _mx(cat0(b_, k_), cat0(u_, v_), P_M, _TN),
                 bh_s[ch], kh_s[ch], u, v_s[ch], m, gcol[ch])
        for p in pairs:
            y_ref[c * L:(c + 1) * L, sls[p]] = y[p][0:L] + y[p][L:2 * L]
    for p in pairs:
        m_scr[p] = m[p]

    @pl.when(j == nsteps - 1)
    def _store():
        mout_ref[0] = m_scr[...]


def _wkv_stage(rw, m0, tri, *, batch, seq):
    assert seq % CHUNK == 0
    nch = max(d for d in (4, 2, 1) if (seq // CHUNK) % d == 0)
    rows = nch * CHUNK
    nsteps = seq // rows
    return pl.pallas_call(
        functools.partial(_wkv_kernel, nsteps=nsteps, nch=nch),
        grid=(batch, nsteps),
        in_specs=[pl.BlockSpec((rows, 6 * RW_WIDTH), lambda b, j: (b * nsteps + j, 0)),
                  pl.BlockSpec((1, HEAD_PAIRS, LANES, LANES), lambda b, j: (b, 0, 0, 0)),
                  _const_spec((CHUNK, CHUNK))],
        out_specs=(pl.BlockSpec((rows, RW_WIDTH), lambda b, j: (b * nsteps + j, 0)),
                   pl.BlockSpec((1, HEAD_PAIRS, LANES, LANES), lambda b, j: (b, 0, 0, 0))),
        out_shape=(jax.ShapeDtypeStruct((batch * seq, RW_WIDTH), F32),
                   jax.ShapeDtypeStruct((batch, HEAD_PAIRS, LANES, LANES), F32)),
        scratch_shapes=[pltpu.VMEM((HEAD_PAIRS, LANES, LANES), F32)],
        compiler_params=pltpu.CompilerParams(dimension_semantics=("parallel", "arbitrary"),
                                             vmem_limit_bytes=VMEM_LIMIT),
        name="wkv",
    )(rw, m0, tri)


def _out_kernel(x_ref, oa_ref, y_ref, bon_ref, ga_ref, gb_ref, ma_ref, mb_ref, v512_ref, seg_ref,
                woa_ref, wob_ref, wo_ref, out_ref):
    v512 = v512_ref[...]
    lnx_w, lnx_b = v512[5:6], v512[6:7]
    seg = seg_ref[...]
    oa = (oa_ref[...] * ga_ref[...]).astype(BF16)
    y = y_ref[...]
    d = y - _seg_sum(y, seg) * (1.0 / RW_HEAD)
    var = _seg_sum(d * d, seg) * (1.0 / RW_HEAD)
    yn = d * lax.rsqrt(var + GN_EPS) * lnx_w + lnx_b
    ob = ((yn + bon_ref[...]) * gb_ref[...]).astype(BF16)
    m = ma_ref[...] * _dot(oa, woa_ref[...]) + mb_ref[...] * _dot(ob, wob_ref[...])
    out_ref[...] = x_ref[...] + _dot(m.astype(BF16), wo_ref[...])


def _out_stage(x2d, oa, y, bon, ga, gb, ma, mb, lw):
    n = x2d.shape[0]
    tm = min(256, n)
    assert n % tm == 0
    row = lambda w: pl.BlockSpec((tm, w), lambda i: (i, 0))
    return pl.pallas_call(
        _out_kernel,
        grid=(n // tm,),
        in_specs=[row(D_MODEL), row(512), row(512), row(512), row(512), row(512), row(D_MODEL), row(D_MODEL),
                  _const_spec((8, RW_WIDTH)), _const_spec((RW_WIDTH, RW_WIDTH)),
                  _const_spec((512, D_MODEL)), _const_spec((512, D_MODEL)), _const_spec((D_MODEL, D_MODEL))],
        out_specs=row(D_MODEL),
        out_shape=jax.ShapeDtypeStruct((n, D_MODEL), F32),
        compiler_params=pltpu.CompilerParams(dimension_semantics=("parallel",), vmem_limit_bytes=VMEM_LIMIT),
        name="out_stage",
    )(x2d, oa, y, bon, ga, gb, ma, mb, lw["v512"], lw["seg"], lw["w_out_a"], lw["w_out_b"], lw["w_o"])


def _swap_halves(w):
    half = ROPE // 2
    return jnp.concatenate([w[..., half:], w[..., :half]], axis=-1)


def _prep_weights(norm_w, w_in, q_norm_w, kv_norm_w, w_uq, w_ukv, qn_nope, qn_rope, kn_nope, kn_rope,
                  mu_shift, w0, w2, a0, a2, k_k, k_a, r_k, lnx_w, lnx_b, w_out_a, w_out_b, w_o):
    depth = w_in.shape[0]
    z = lambda *s: jnp.zeros((depth,) + s, F32)
    kpe = w_in[..., 384:416]
    w_lat = jnp.concatenate([w_in[..., :384], kpe, z(D_MODEL, 96), _swap_halves(kpe), z(D_MODEL, 96)],
                            axis=-1).astype(BF16)
    w_in_p = w_in[..., 416:].astype(BF16)
    uq = w_uq.reshape(depth, Q_RANK, HEADS, QK)
    uq_p = jnp.concatenate([uq, z(Q_RANK, HEADS, 32)], axis=-1).reshape(depth, Q_RANK, HEADS * LANES)
    uq_s = jnp.concatenate([z(Q_RANK, HEADS, NOPE), _swap_halves(uq[..., NOPE:]), z(Q_RANK, HEADS, 32)],
                           axis=-1).reshape(depth, Q_RANK, HEADS * LANES)
    ukv = w_ukv.reshape(depth, KV_RANK, HEADS, 2 * NOPE)
    zk = z(KV_RANK, HEADS, NOPE)
    k_part = jnp.concatenate([ukv[..., :NOPE], zk], axis=-1)
    vv = ukv[..., NOPE:]
    odd = (jnp.arange(HEADS) % 2 == 1)[None, None, :, None]
    v_part = jnp.where(odd, jnp.concatenate([zk, vv], axis=-1), jnp.concatenate([vv, zk], axis=-1))
    ukv_p = jnp.concatenate([k_part.reshape(depth, KV_RANK, HEADS * LANES),
                             v_part.reshape(depth, KV_RANK, HEADS * LANES)], axis=-1).astype(BF16)
    zg = lambda n: jnp.zeros((depth, n), F32)
    g128 = jnp.stack([
        jnp.concatenate([qn_nope, qn_rope, zg(32)], axis=-1),
        jnp.concatenate([zg(NOPE), _swap_halves(qn_rope), zg(32)], axis=-1),
        jnp.concatenate([kn_rope, zg(96)], axis=-1),
        jnp.concatenate([_swap_halves(kn_rope), zg(96)], axis=-1),
        kv_norm_w,
        jnp.concatenate([kn_nope, zg(NOPE)], axis=-1),
        zg(LANES), zg(LANES)], axis=1)
    v512 = jnp.stack([w0, a0, k_k, k_a, r_k.reshape(depth, RW_WIDTH), lnx_w, lnx_b, zg(RW_WIDTH)], axis=1)
    seg_id = jnp.arange(RW_WIDTH) // RW_HEAD
    seg = (seg_id[:, None] == seg_id[None, :]).astype(BF16)
    onecol = jnp.stack([jnp.arange(LANES) == NOPE, jnp.arange(LANES) == 0]).astype(F32)
    return dict(
        norm_w=norm_w[:, None, :], w_lat=w_lat, w_in=w_in_p, q_norm_w=q_norm_w[:, None, :],
        w_uq=uq_p.astype(BF16), w_uq_sw=uq_s.astype(BF16), w_ukv=ukv_p, g128=g128, mu=mu_shift[:, None, :],
        v512=v512, w2=jnp.concatenate([w2, z(LORA, RW_WIDTH)], axis=1),
        a2=jnp.concatenate([z(LORA, RW_WIDTH), a2], axis=1),
        w_out_a=w_out_a.astype(BF16), w_out_b=w_out_b.astype(BF16), w_o=w_o.astype(BF16),
    ), dict(seg=seg, onecol=onecol)


def _rope_table(past, seq, batch):
    half = ROPE // 2
    inv = ROPE_THETA ** (-jnp.arange(half, dtype=F32) / half)
    pos = past + jnp.arange(seq, dtype=jnp.int32)
    ang = pos.astype(F32)[:, None] * inv[None, :]
    cos, sin = jnp.cos(ang), jnp.sin(ang)
    one, zero = jnp.ones((seq, NOPE), F32), jnp.zeros((seq, NOPE), F32)
    z32, z96 = jnp.zeros((seq, 32), F32), jnp.zeros((seq, 96), F32)
    tab = jnp.concatenate([one, cos, cos, z32, zero, -sin, sin, z32, cos, cos, z96, -sin, sin, z96], axis=-1)
    return jnp.tile(tab, (batch, 1))


def _state_to_pairs(s):
    b = s.shape[0]
    m = jnp.swapaxes(s.astype(F32), -1, -2).reshape(b, HEAD_PAIRS, 2, RW_HEAD, RW_HEAD)
    zero = jnp.zeros_like(m[:, :, 0])
    top = jnp.concatenate([m[:, :, 0], zero], axis=-1)
    bot = jnp.concatenate([zero, m[:, :, 1]], axis=-1)
    return jnp.concatenate([top, bot], axis=-2)


def _pairs_to_state(m):
    b = m.shape[0]
    e = m[:, :, :RW_HEAD, :RW_HEAD]
    o = m[:, :, RW_HEAD:, RW_HEAD:]
    return jnp.swapaxes(jnp.stack([e, o], axis=2).reshape(b, HEADS, RW_HEAD, RW_HEAD), -1, -2)


def _round_up(n, m):
    return (n + m - 1) // m * m


def _group_layer(x2d, tab, lw, ckv_past, kpe_past, m0, sh0, tri, *, batch, seq, past):
    n = batch * seq
    q, ckv, kpe, rw, bon, ga, gb, ma, mb, shift = _in_stage(x2d, tab, sh0, lw, seq=seq)
    ckv3 = ckv.reshape(batch, seq, KV_RANK)
    kpe3 = kpe.reshape(batch, seq, ROPE)
    kv_len = past + seq
    if past > 0:
        ckv_all = jnp.concatenate([ckv_past, ckv3], axis=1)
        kpe_all = jnp.concatenate([kpe_past, kpe3], axis=1)
    else:
        ckv_all, kpe_all = ckv3, kpe3
    if seq >= 512:
        tq, tk = (1024 if seq % 1024 == 0 else 512), 512
        kv_pad = _round_up(kv_len, tk)
    else:
        tq = seq
        kv_pad = _round_up(kv_len, LANES)
        tk = kv_pad
    ckv_all = jnp.pad(ckv_all, ((0, 0), (0, kv_pad - kv_len), (0, 0)))
    kpe_all = jnp.pad(kpe_all, ((0, 0), (0, kv_pad - kv_len), (NOPE, LANES - QK)))
    kv_tile = max(t for t in (128, 256, 384, 512, 768, 1024, 1408) if kv_pad % t == 0)
    k_att, v_att = _kv_stage(ckv_all.reshape(batch * kv_pad, KV_RANK), kpe_all.reshape(batch * kv_pad, LANES),
                             lw, tile=kv_tile)
    oa = _attn_stage(q, k_att, v_att, batch=batch, seq=seq, past=past, kv_len=kv_len, kv_pad=kv_pad, tq=tq, tk=tk)
    y, m1 = _wkv_stage(rw, m0, tri, batch=batch, seq=seq)
    out = _out_stage(x2d, oa, y, bon, ga, gb, ma, mb, lw)
    return out, ckv3, kpe3, _pairs_to_state(m1), shift.reshape(batch, 1, SHIFT_WIDTH)


def kernel(x_prompt, x_sample, cache_ckv, cache_kpe, state_wkv, state_shift, norm_w, w_in, q_norm_w, kv_norm_w,
           w_uq, w_ukv, qn_nope, qn_rope, kn_nope, kn_rope, mu_shift, w0, w2, a0, a2, k_k, k_a, r_k, lnx_w, lnx_b,
           w_out_a, w_out_b, w_o):
    depth = w_in.shape[0]
    bp, tp, _ = x_prompt.shape
    bs, ts, _ = x_sample.shape
    past = cache_ckv.shape[2]
    assert past % CHUNK == 0 and tp % CHUNK == 0 and ts % CHUNK == 0
    stacked, shared = _prep_weights(norm_w, w_in, q_norm_w, kv_norm_w, w_uq, w_ukv, qn_nope, qn_rope, kn_nope,
                                    kn_rope, mu_shift, w0, w2, a0, a2, k_k, k_a, r_k, lnx_w, lnx_b,
                                    w_out_a, w_out_b, w_o)
    tab_p = _rope_table(0, tp, bp)
    tab_s = _rope_table(past, ts, bs)
    tri = (jnp.arange(CHUNK)[:, None] >= jnp.arange(CHUNK)[None, :]).astype(BF16)
    zero_m = jnp.zeros((bp, HEAD_PAIRS, LANES, LANES), F32)
    zero_sh = jnp.zeros((bp, 1, SHIFT_WIDTH), F32)

    yp = x_prompt.reshape(bp * tp, D_MODEL)
    ys = x_sample.reshape(bs * ts, D_MODEL)
    outs_p, outs_s = [], []
    for i in range(depth):
        lw = {name: val[i] for name, val in stacked.items()}
        lw.update(shared)
        yp, *rest_p = _group_layer(yp, tab_p, lw, None, None, zero_m, zero_sh, tri, batch=bp, seq=tp, past=0)
        outs_p.append(rest_p)
        ys, *rest_s = _group_layer(ys, tab_s, lw, cache_ckv[i], cache_kpe[i], _state_to_pairs(state_wkv[i]),
                                   state_shift[i], tri, batch=bs, seq=ts, past=past)
        outs_s.append(rest_s)
    stack = lambda outs, j: jnp.stack([o[j] for o in outs])
    return (yp.reshape(bp, tp, D_MODEL), ys.reshape(bs, ts, D_MODEL),
            stack(outs_p, 0), stack(outs_p, 1), stack(outs_p, 2), stack(outs_p, 3),
            stack(outs_s, 0), stack(outs_s, 1), stack(outs_s, 2), stack(outs_s, 3))
```

```python
import functools

import jax
import jax.numpy as jnp
from jax import lax
from jax.experimental import pallas as pl
from jax.experimental.pallas import tpu as pltpu

F32 = jnp.float32
BF16 = jnp.bfloat16
HI = lax.Precision.HIGHEST

D_MODEL = 1024
CHUNK = 64
CHUNK_SHIFT = 6
EPS = 1e-6
GN_EPS = 64e-5
HEADS = 8
NOPE = 64
ROPE = 32
QK = NOPE + ROPE
Q_RANK = 256
KV_RANK = 128
ROPE_THETA = 10000.0
RW_WIDTH = 512
RW_HEAD = 64
LORA = 64
SHIFT_WIDTH = 3 * RW_WIDTH + 2 * LORA
LANES = 128
HEAD_PAIRS = HEADS // 2
NEG = -1e30
P_PROD, P_INV, P_W, P_U, P_Y, P_M = 1, 1, 1, 1, 1, 1
LOG2_E = 1.4426950408889634

W_LAT = 640
C_GA = 0
C_ZS = C_GA + 512
C_GB = C_ZS + SHIFT_WIDTH
C_MA = C_GB + 512
C_MB = C_MA + D_MODEL
C_END = C_MB + D_MODEL

VMEM_LIMIT = 56 * 1024 * 1024


def _const_spec(shape):
    n = len(shape)
    return pl.BlockSpec(shape, lambda *_: (0,) * n)


_NN = (((1,), (0,)), ((), ()))
_NT = (((1,), (1,)), ((), ()))
_TN = (((0,), (0,)), ((), ()))


def _dot(a, b):
    return jnp.dot(a, b, preferred_element_type=F32)


def _dot_nt(a, b):
    return lax.dot_general(a, b, _NT, preferred_element_type=F32)


def _split(x):
    hi = x.astype(BF16)
    return hi, (x - hi.astype(F32)).astype(BF16)


def _mx(a, b, passes, dims=_NN):
    if passes == 6:
        return lax.dot_general(a, b, dims, precision=HI, preferred_element_type=F32)
    dg = lambda x, y: lax.dot_general(x, y, dims, preferred_element_type=F32)
    if passes == 1:
        return dg(a.astype(BF16), b.astype(BF16))
    ah, al = _split(a)
    bh, bl = _split(b)
    return dg(ah, bh) + (dg(ah, bl) + dg(al, bh))


def _seg_sum(x, seg, passes=2):
    hi = x.astype(BF16)
    if passes == 1:
        return _dot(hi, seg)
    lo = (x - hi.astype(F32)).astype(BF16)
    return _dot(hi, seg) + _dot(lo, seg)


def _in_kernel(x_ref, tab_ref, sh0_ref, normw_ref, wlat_ref, win_ref, qnw_ref, wuq_ref, wuqs_ref, g128_ref, mu_ref,
               v512_ref, w2_ref, a2_ref, seg_ref,
               q_ref, ckv_ref, kpe_ref, rw_ref, bon_ref, ga_ref, gb_ref, ma_ref, mb_ref, shift_ref,
               carry_ref, *, tm, seq, scale):
    i = pl.program_id(0)
    x = x_ref[...]
    h = x * lax.rsqrt(jnp.mean(x * x, axis=-1, keepdims=True) + EPS) * normw_ref[...]
    hb = h.astype(BF16)
    tab = tab_ref[...]
    g128 = g128_ref[...]

    z1 = _dot(hb, wlat_ref[...])
    cq = z1[:, :Q_RANK]
    cqn = cq * lax.rsqrt(jnp.mean(cq * cq, axis=-1, keepdims=True) + EPS) * qnw_ref[...]
    kvl = z1[:, Q_RANK:Q_RANK + KV_RANK]
    ckv_ref[...] = kvl * lax.rsqrt(jnp.mean(kvl * kvl, axis=-1, keepdims=True) + EPS) * g128[4:5]
    t1 = z1[:, 384:512]
    t2 = z1[:, 512:640]
    kinv = lax.rsqrt(jnp.sum(t1 * t1, axis=-1, keepdims=True) * (1.0 / ROPE) + EPS)
    kr = (t1 * g128[2:3] * tab[:, 256:384] + t2 * g128[3:4] * tab[:, 384:512]) * kinv
    kpe_ref[...] = kr[:, :ROPE]

    cqb = cqn.astype(BF16)
    qr = _dot(cqb, wuq_ref[...])
    qs = _dot(cqb, wuqs_ref[...])
    lane = lax.broadcasted_iota(jnp.int32, (1, LANES), 1)
    is_nope = lane < NOPE
    cos_f = tab[:, 0:128]
    sin_f = tab[:, 128:256]
    for hd in range(HEADS):
        t = qr[:, hd * LANES:(hd + 1) * LANES]
        sq = t * t
        ms_n = jnp.sum(jnp.where(is_nope, sq, 0.0), axis=-1, keepdims=True) * (1.0 / NOPE)
        ms_r = jnp.sum(jnp.where(is_nope, 0.0, sq), axis=-1, keepdims=True) * (1.0 / ROPE)
        inv = jnp.where(is_nope, lax.rsqrt(ms_n + EPS), lax.rsqrt(ms_r + EPS))
        qh = (t * g128[0:1] * cos_f + qs[:, hd * LANES:(hd + 1) * LANES] * g128[1:2] * sin_f) * (inv * scale)
        q_ref[hd] = qh.astype(BF16)

    ga_ref[...] = jax.nn.silu(_dot(hb, win_ref[:, C_GA:C_ZS])).astype(BF16)
    gb_ref[...] = jax.nn.silu(_dot(hb, win_ref[:, C_GB:C_MA])).astype(BF16)
    ma_ref[...] = jax.nn.sigmoid(_dot(hb, win_ref[:, C_MA:C_MB])).astype(BF16)
    mb_ref[...] = jax.nn.sigmoid(_dot(hb, win_ref[:, C_MB:C_END])).astype(BF16)

    zs = _dot(hb, win_ref[:, C_ZS:C_GB])
    rows = lax.broadcasted_iota(jnp.int32, (tm, 1), 0)
    prev = pltpu.roll(zs, 1, axis=0)
    if seq >= tm:
        tiles_per_seq = seq // tm

        @pl.when(i == 0)
        def _init_carry():
            carry_ref[...] = jnp.zeros(carry_ref.shape, F32)

        first = jnp.where(i % tiles_per_seq == 0, sh0_ref[0], carry_ref[...])
        prev = jnp.where(rows == 0, first, prev)
        carry_ref[...] = zs[tm - 1:tm, :]
        shift_ref[0] = zs[tm - 1:tm, :]
    else:
        for s in range(tm // seq):
            prev = jnp.where(rows == s * seq, sh0_ref[0, s:s + 1, :], prev)
            shift_ref[0, s:s + 1, :] = zs[(s + 1) * seq - 1:(s + 1) * seq, :]
    zm = zs + (prev - zs) * mu_ref[...]
    r = zm[:, 0:512]
    k = zm[:, 512:1024]
    v = zm[:, 1024:1536]
    t13 = zm[:, 1536:1664]
    v512 = v512_ref[...]
    w0, a0, k_k, k_a, r_k = v512[0:1], v512[1:2], v512[2:3], v512[3:4], v512[4:5]
    seg = seg_ref[...]
    log_w = -jax.nn.softplus(-(w0 + _mx(jnp.tanh(t13), w2_ref[...], 3))) - 0.5
    lw = -jnp.exp(log_w)
    aic = jax.nn.sigmoid(a0 + _mx(t13, a2_ref[...], 3))
    kkr = k * k_k
    kk = kkr / jnp.maximum(jnp.sqrt(_seg_sum(kkr * kkr, seg, 1)), 1e-12)
    k2 = k * (1.0 + (aic - 1.0) * k_a)
    bon_ref[...] = (_seg_sum(r * k2 * r_k, seg, 1) * v).astype(BF16)
    rw_ref[:, 0:512] = r
    rw_ref[:, 512:1024] = lw
    rw_ref[:, 1024:1536] = k2
    rw_ref[:, 1536:2048] = v
    rw_ref[:, 2048:2560] = -kk
    rw_ref[:, 2560:3072] = kk * aic


def _in_stage(x2d, tab, sh0, lw, *, seq):
    n = x2d.shape[0]
    tm = 256 if seq >= 256 else min(256, n)
    assert n % tm == 0 and (seq % tm == 0 or tm % seq == 0)
    nseq = max(1, tm // seq)
    batch = n // seq
    sh0 = sh0.reshape(batch // nseq, nseq, SHIFT_WIDTH)
    if seq >= tm:
        sh_map = lambda i: (i // (seq // tm), 0, 0)
    else:
        sh_map = lambda i: (i, 0, 0)
    row = lambda w: pl.BlockSpec((tm, w), lambda i: (i, 0))
    out_shape = (
        jax.ShapeDtypeStruct((HEADS, n, LANES), BF16),
        jax.ShapeDtypeStruct((n, KV_RANK), F32),
        jax.ShapeDtypeStruct((n, ROPE), F32),
        jax.ShapeDtypeStruct((n, 6 * RW_WIDTH), F32),
        jax.ShapeDtypeStruct((n, RW_WIDTH), BF16),
        jax.ShapeDtypeStruct((n, 512), BF16),
        jax.ShapeDtypeStruct((n, 512), BF16),
        jax.ShapeDtypeStruct((n, D_MODEL), BF16),
        jax.ShapeDtypeStruct((n, D_MODEL), BF16),
        jax.ShapeDtypeStruct((batch // nseq, nseq, SHIFT_WIDTH), F32),
    )
    out_specs = (
        pl.BlockSpec((HEADS, tm, LANES), lambda i: (0, i, 0)),
        row(KV_RANK), row(ROPE), row(6 * RW_WIDTH), row(RW_WIDTH), row(512), row(512), row(D_MODEL), row(D_MODEL),
        pl.BlockSpec((1, nseq, SHIFT_WIDTH), sh_map),
    )
    in_specs = [
        row(D_MODEL), row(512), pl.BlockSpec((1, nseq, SHIFT_WIDTH), sh_map),
        _const_spec((1, D_MODEL)), _const_spec((D_MODEL, W_LAT)),
        pl.BlockSpec((D_MODEL, C_END), lambda i: (0, 0), pipeline_mode=pl.Buffered(1)),
        _const_spec((1, Q_RANK)), _const_spec((Q_RANK, HEADS * LANES)), _const_spec((Q_RANK, HEADS * LANES)),
        _const_spec((8, LANES)), _const_spec((1, SHIFT_WIDTH)), _const_spec((8, RW_WIDTH)),
        _const_spec((LANES, RW_WIDTH)), _const_spec((LANES, RW_WIDTH)), _const_spec((RW_WIDTH, RW_WIDTH)),
    ]
    return pl.pallas_call(
        functools.partial(_in_kernel, tm=tm, seq=seq, scale=QK ** -0.5 * LOG2_E),
        grid=(n // tm,),
        in_specs=in_specs,
        out_specs=out_specs,
        out_shape=out_shape,
        scratch_shapes=[pltpu.VMEM((1, SHIFT_WIDTH), F32)],
        compiler_params=pltpu.CompilerParams(dimension_semantics=("arbitrary",), vmem_limit_bytes=VMEM_LIMIT),
        name="in_stage",
    )(x2d, tab, sh0, lw["norm_w"], lw["w_lat"], lw["w_in"], lw["q_norm_w"], lw["w_uq"], lw["w_uq_sw"], lw["g128"], lw["mu"],
      lw["v512"], lw["w2"], lw["a2"], lw["seg"])


def _head_keys_values(kv, kpe, gkn, one):
    kr = kv[:, :LANES]
    inv = lax.rsqrt(jnp.sum(kr * kr, axis=-1, keepdims=True) * (1.0 / NOPE) + EPS)
    return (kr * inv * gkn + kpe).astype(BF16), (kv[:, LANES:] + one).astype(BF16)


def _kv_kernel(ckv_ref, kpe_ref, wukv_ref, g128_ref, one_ref, place_ref, k_ref, v_ref):
    ckv = ckv_ref[...].astype(BF16)
    kpe = _dot(kpe_ref[...].astype(BF16), place_ref[...])
    for hd in range(HEADS):
        k_ref[hd], v_ref[hd] = _head_keys_values(_dot(ckv, wukv_ref[hd]), kpe, g128_ref[5:6],
                                                 one_ref[hd % 2:hd % 2 + 1])


def _kv_stage(ckv2d, kpe2d, lw, *, tile):
    n = ckv2d.shape[0]
    assert n % tile == 0
    hspec = pl.BlockSpec((HEADS, tile, LANES), lambda i: (0, i, 0))
    return pl.pallas_call(
        _kv_kernel,
        grid=(n // tile,),
        in_specs=[pl.BlockSpec((tile, KV_RANK), lambda i: (i, 0)), pl.BlockSpec((tile, ROPE), lambda i: (i, 0)),
                  _const_spec((HEADS, KV_RANK, 2 * LANES)), _const_spec((8, LANES)), _const_spec((2, LANES)),
                  _const_spec((ROPE, LANES))],
        out_specs=(hspec, hspec),
        out_shape=(jax.ShapeDtypeStruct((HEADS, n, LANES), BF16),) * 2,
        compiler_params=pltpu.CompilerParams(dimension_semantics=("parallel",), vmem_limit_bytes=VMEM_LIMIT),
        name="kv_stage",
    )(ckv2d, kpe2d, lw["w_ukv"], lw["g128"], lw["onecol"], lw["place"])


def _decode_kernel(q_ref, ckvp_ref, kpep_ref, ckvn_ref, kpen_ref, wukv_ref, g128_ref, one_ref, place_ref, o_ref):
    place = place_ref[...]
    gkn = g128_ref[5:6]
    ckv_p = ckvp_ref[...].astype(BF16)
    ckv_n = ckvn_ref[...].astype(BF16)
    kpe_p = _dot(kpep_ref[...].astype(BF16), place)
    kpe_n = _dot(kpen_ref[...].astype(BF16), place)

    def keys_values(hd):
        one = one_ref[hd % 2:hd % 2 + 1]
        w = wukv_ref[hd]
        return (_head_keys_values(_dot(ckv_p, w), kpe_p, gkn, one) +
                _head_keys_values(_dot(ckv_n, w), kpe_n, gkn, one))

    acc = []
    nxt = keys_values(0)
    for hd in range(HEADS):
        kp, vp, kn, vn = nxt
        q = q_ref[hd]
        sp = _dot_nt(q, kp)
        sn = _dot_nt(q, kn)
        if hd + 1 < HEADS:
            nxt = keys_values(hd + 1)
        m = jnp.maximum(jnp.max(sp, axis=-1, keepdims=True), jnp.max(sn, axis=-1, keepdims=True))
        acc.append(_dot(jnp.exp2((sp - m).astype(BF16)), vp) + _dot(jnp.exp2((sn - m).astype(BF16)), vn))
    lane = lax.broadcasted_iota(jnp.int32, (1, LANES), 1)
    for p in range(HEAD_PAIRS):
        ae, ao = acc[2 * p], acc[2 * p + 1]
        o_ref[:, p * LANES:(p + 1) * LANES] = jnp.where(lane < 64, ae / ae[:, 64:65], ao / ao[:, 0:1])


def _decode_stage(q, ckv_past2d, kpe_past2d, past_block, ckv_new, kpe_new, lw, *, batch, seq, past):
    assert seq == CHUNK and past % CHUNK == 0
    return pl.pallas_call(
        _decode_kernel,
        grid=(batch,),
        in_specs=[pl.BlockSpec((HEADS, seq, LANES), lambda b: (0, b, 0)),
                  pl.BlockSpec((past, KV_RANK), lambda b: (past_block + b, 0)),
                  pl.BlockSpec((past, ROPE), lambda b: (past_block + b, 0)),
                  pl.BlockSpec((seq, KV_RANK), lambda b: (b, 0)),
                  pl.BlockSpec((seq, ROPE), lambda b: (b, 0)),
                  _const_spec((HEADS, KV_RANK, 2 * LANES)), _const_spec((8, LANES)), _const_spec((2, LANES)),
                  _const_spec((ROPE, LANES))],
        out_specs=pl.BlockSpec((seq, HEADS * NOPE), lambda b: (b, 0)),
        out_shape=jax.ShapeDtypeStruct((batch * seq, HEADS * NOPE), F32),
        compiler_params=pltpu.CompilerParams(dimension_semantics=("parallel",), vmem_limit_bytes=VMEM_LIMIT),
        name="decode_attention",
    )(q, ckv_past2d, kpe_past2d, ckv_new, kpe_new, lw["w_ukv"], lw["g128"], lw["onecol"], lw["place"])


def _last_key_tile(qi, *, tq, tk, past):
    return (past + qi * tq + tq - 1) // tk


def _attn_kernel(qi_ref, kj_ref, q_ref, k_ref, v_ref, o_ref, m_scr, acc_scr, *, tq, tk, past, kv_len, nk):
    t = pl.program_id(1)
    qi = qi_ref[t]
    kj = kj_ref[t]

    @pl.when(kj == 0)
    def _init():
        m_scr[...] = jnp.full(m_scr.shape, NEG, F32)
        acc_scr[...] = jnp.zeros(acc_scr.shape, F32)

    q_first = past + qi * tq
    k_first = kj * tk
    k_last = k_first + tk - 1
    unmasked = jnp.logical_and((k_last >> CHUNK_SHIFT) <= (q_first >> CHUNK_SHIFT), k_last < kv_len)

    def compute(masked):
        if masked:
            rows = q_first + lax.broadcasted_iota(jnp.int32, (tq, tk), 0)
            cols = k_first + lax.broadcasted_iota(jnp.int32, (tq, tk), 1)
            ok = jnp.logical_and((cols >> CHUNK_SHIFT) <= (rows >> CHUNK_SHIFT), cols < kv_len)
        s_next = _dot_nt(q_ref[0], k_ref[0])
        for hd in range(HEADS):
            s = s_next
            if hd + 1 < HEADS:
                s_next = _dot_nt(q_ref[hd + 1], k_ref[hd + 1])
            if masked:
                s = jnp.where(ok, s, NEG)
            m_prev = m_scr[hd]
            m_new = jnp.maximum(m_prev, jnp.max(s, axis=-1, keepdims=True))
            alpha = jnp.exp2(m_prev - m_new)
            p = jnp.exp2((s - jnp.concatenate([m_new] * (tk // LANES), axis=1)).astype(BF16))
            acc_scr[hd] = acc_scr[hd] * alpha + _dot(p, v_ref[hd])
            m_scr[hd] = m_new

    @pl.when(unmasked)
    def _full():
        compute(False)

    @pl.when(jnp.logical_not(unmasked))
    def _diag():
        compute(True)

    @pl.when(kj == jnp.minimum(_last_key_tile(qi, tq=tq, tk=tk, past=past), nk - 1))
    def _finish():
        lane = lax.broadcasted_iota(jnp.int32, (1, LANES), 1)
        for p in range(HEAD_PAIRS):
            ae = acc_scr[2 * p]
            ao = acc_scr[2 * p + 1]
            o_ref[:, p * LANES:(p + 1) * LANES] = jnp.where(lane < 64, ae / ae[:, 64:65], ao / ao[:, 0:1])


def _attn_stage(q, k, v, *, batch, seq, past, kv_len, kv_pad, tq, tk):
    nq = seq // tq
    nk = kv_pad // tk
    assert seq % tq == 0 and kv_pad % tk == 0 and tq % CHUNK == 0 and past % CHUNK == 0
    steps = [(qi, kj) for qi in range(nq)
             for kj in range(min(_last_key_tile(qi, tq=tq, tk=tk, past=past), nk - 1) + 1)]
    qi_tab = jnp.asarray([s[0] for s in steps], jnp.int32)
    kj_tab = jnp.asarray([s[1] for s in steps], jnp.int32)
    kv_map = lambda b, t, qi, kj: (0, b * nk + kj[t], 0)
    grid_spec = pltpu.PrefetchScalarGridSpec(
        num_scalar_prefetch=2,
        grid=(batch, len(steps)),
        in_specs=[pl.BlockSpec((HEADS, tq, LANES), lambda b, t, qi, kj: (0, b * nq + qi[t], 0)),
                  pl.BlockSpec((HEADS, tk, LANES), kv_map),
                  pl.BlockSpec((HEADS, tk, LANES), kv_map)],
        out_specs=pl.BlockSpec((tq, HEADS * NOPE), lambda b, t, qi, kj: (b * nq + qi[t], 0)),
        scratch_shapes=[pltpu.VMEM((HEADS, tq, LANES), F32), pltpu.VMEM((HEADS, tq, LANES), F32)],
    )
    return pl.pallas_call(
        functools.partial(_attn_kernel, tq=tq, tk=tk, past=past, kv_len=kv_len, nk=nk),
        grid_spec=grid_spec,
        out_shape=jax.ShapeDtypeStruct((batch * seq, HEADS * NOPE), F32),
        compiler_params=pltpu.CompilerParams(dimension_semantics=("parallel", "arbitrary"),
                                             vmem_limit_bytes=VMEM_LIMIT),
        name="attention",
    )(qi_tab, kj_tab, q, k, v)


def _wkv_kernel(rw_ref, m0_ref, tri_ref, y_ref, mout_ref, m_scr, *, nsteps, nch):
    j = pl.program_id(1)

    @pl.when(j == 0)
    def _load():
        m_scr[...] = m0_ref[0]

    L = CHUNK
    lane = lax.broadcasted_iota(jnp.int32, (L, LANES), 1)
    even = lane < RW_HEAD
    ri = lax.broadcasted_iota(jnp.int32, (LANES, LANES), 0)
    ci = lax.broadcasted_iota(jnp.int32, (LANES, LANES), 1)
    tr = ri & (L - 1)
    tc = ci & (L - 1)
    strict = tr > tc
    incl = tr >= tc
    blk16 = (ri >> 4) == (ci >> 4)
    ident = jnp.where(ri == ci, 1.0, 0.0).astype(F32)

    def stack(t):
        return jnp.concatenate([jnp.where(even, t, 0.0), jnp.where(even, 0.0, t)], axis=0)

    pairs = range(HEAD_PAIRS)
    each = lambda fn, *cols: [fn(*args) for args in zip(*cols)]
    cat0 = lambda *xs: jnp.concatenate(xs, axis=0)
    cat1 = lambda *xs: jnp.concatenate(xs, axis=1)
    sls = [slice(p * LANES, (p + 1) * LANES) for p in pairs]

    a_s, r_s, b_s, k_s, bh_s, kh_s, v_s, gcol = [], [], [], [], [], [], [], []
    tri = tri_ref[...]
    for c in range(nch):
        rows = slice(c * L, (c + 1) * L)
        r = rw_ref[rows, 0:512]
        lw = rw_ref[rows, 512:1024]
        k = rw_ref[rows, 1024:1536]
        v = rw_ref[rows, 1536:2048]
        a = rw_ref[rows, 2048:2560]
        b = rw_ref[rows, 2560:3072]
        lw_hi, lw_rest = lw.astype(BF16), lw - lw.astype(BF16).astype(F32)
        lw_mid = lw_rest.astype(BF16)
        lw_lo = (lw_rest - lw_mid.astype(F32)).astype(BF16)
        cum = _dot(tri, lw_hi) + (_dot(tri, lw_mid) + _dot(tri, lw_lo))
        gt = cum[L - 1:L, :]
        en = jnp.exp(-cum)
        eh = jnp.exp(gt - cum)
        egt = jnp.exp(gt)
        for dst, val in ((a_s, a * jnp.exp(cum - lw)), (r_s, r * jnp.exp(cum)), (b_s, b * en), (k_s, k * en),
                         (bh_s, b * eh), (kh_s, k * eh), (v_s, v)):
            dst.extend(stack(val[:, sl]) for sl in sls)
        gcol.extend(jnp.transpose(jnp.broadcast_to(egt[:, sl], (LANES, LANES))) for sl in sls)
    prod = each(lambda a_, r_, b_, k_: _mx(cat0(a_, r_), cat0(b_, k_), P_PROD, _NT), a_s, r_s, b_s, k_s)
    n_ab = [jnp.where(strict, x[0:LANES, 0:LANES], 0.0) for x in prod]
    a_ak = [jnp.where(strict, x[0:LANES, LANES:2 * LANES], 0.0) for x in prod]
    a_rb = [jnp.where(incl, x[LANES:2 * LANES, 0:LANES], 0.0) for x in prod]
    a_rk = [jnp.where(incl, x[LANES:2 * LANES, LANES:2 * LANES], 0.0) for x in prod]
    inv = lambda x, y_: _mx(x, y_, P_INV)
    d1 = [jnp.where(blk16, x, 0.0) for x in n_ab]
    e1 = each(lambda x, d: x - d, n_ab, d1)
    p1 = [ident + d for d in d1]
    d2 = each(inv, d1, d1)
    p2 = each(lambda p_, d: p_ + inv(p_, d), p1, d2)
    d4 = each(inv, d2, d2)
    p3 = each(lambda p_, d: p_ + inv(p_, d), p2, d4)
    d8 = each(inv, d4, d4)
    t_d = each(lambda p_, d: p_ + inv(p_, d), p3, d8)
    f1 = each(inv, t_d, e1)
    f2 = each(inv, f1, f1)
    if1 = [ident + f for f in f1]
    g1 = each(lambda i_, f: i_ + inv(i_, f), if1, f2)
    t_m = each(inv, g1, t_d)

    m = [m_scr[p] for p in pairs]
    for c in range(nch):
        ch = slice(c * HEAD_PAIRS, (c + 1) * HEAD_PAIRS)
        w = each(lambda a_, ak, m_, v_: _mx(cat1(a_, ak), cat0(m_, v_), P_W), a_s[ch], a_ak[ch], m, v_s[ch])
        u = each(lambda t_, w_: _mx(t_, w_, P_U), t_m[ch], w)
        y = each(lambda r_, rb, rk, m_, u_, v_: _mx(cat1(r_, rb, rk), cat0(m_, u_, v_), P_Y),
                 r_s[ch], a_rb[ch], a_rk[ch], m, u, v_s[ch])
        m = each(lambda b_, k_, u_, v_, m_, g_: m_ * g_ + _mx(cat0(b_, k_), cat0(u_, v_), P_M, _TN),
                 bh_s[ch], kh_s[ch], u, v_s[ch], m, gcol[ch])
        for p in pairs:
            y_ref[c * L:(c + 1) * L, sls[p]] = y[p][0:L] + y[p][L:2 * L]
    for p in pairs:
        m_scr[p] = m[p]

    @pl.when(j == nsteps - 1)
    def _store():
        mout_ref[0] = m_scr[...]


def _wkv_stage(rw, m0, tri, *, batch, seq):
    assert seq % CHUNK == 0
    nch = max(d for d in (4, 2, 1) if (seq // CHUNK) % d == 0)
    rows = nch * CHUNK
    nsteps = seq // rows
    return pl.pallas_call(
        functools.partial(_wkv_kernel, nsteps=nsteps, nch=nch),
        grid=(batch, nsteps),
        in_specs=[pl.BlockSpec((rows, 6 * RW_WIDTH), lambda b, j: (b * nsteps + j, 0)),
                  pl.BlockSpec((1, HEAD_PAIRS, LANES, LANES), lambda b, j: (b, 0, 0, 0)),
                  _const_spec((CHUNK, CHUNK))],
        out_specs=(pl.BlockSpec((rows, RW_WIDTH), lambda b, j: (b * nsteps + j, 0)),
                   pl.BlockSpec((1, HEAD_PAIRS, LANES, LANES), lambda b, j: (b, 0, 0, 0))),
        out_shape=(jax.ShapeDtypeStruct((batch * seq, RW_WIDTH), F32),
                   jax.ShapeDtypeStruct((batch, HEAD_PAIRS, LANES, LANES), F32)),
        scratch_shapes=[pltpu.VMEM((HEAD_PAIRS, LANES, LANES), F32)],
        compiler_params=pltpu.CompilerParams(dimension_semantics=("parallel", "arbitrary"),
                                             vmem_limit_bytes=VMEM_LIMIT),
        name="wkv",
    )(rw, m0, tri)


def _out_kernel(x_ref, oa_ref, y_ref, bon_ref, ga_ref, gb_ref, ma_ref, mb_ref, v512_ref, seg_ref,
                woa_ref, wob_ref, wo_ref, out_ref):
    v512 = v512_ref[...]
    lnx_w, lnx_b = v512[5:6], v512[6:7]
    seg = seg_ref[...]
    oa = (oa_ref[...] * ga_ref[...]).astype(BF16)
    y = y_ref[...]
    d = y - _seg_sum(y, seg) * (1.0 / RW_HEAD)
    var = _seg_sum(d * d, seg) * (1.0 / RW_HEAD)
    yn = d * lax.rsqrt(var + GN_EPS) * lnx_w + lnx_b
    ob = ((yn + bon_ref[...]) * gb_ref[...]).astype(BF16)
    m = ma_ref[...] * _dot(oa, woa_ref[...]) + mb_ref[...] * _dot(ob, wob_ref[...])
    out_ref[...] = x_ref[...] + _dot(m.astype(BF16), wo_ref[...])


def _out_stage(x2d, oa, y, bon, ga, gb, ma, mb, lw):
    n = x2d.shape[0]
    tm = min(256, n)
    assert n % tm == 0
    row = lambda w: pl.BlockSpec((tm, w), lambda i: (i, 0))
    return pl.pallas_call(
        _out_kernel,
        grid=(n // tm,),
        in_specs=[row(D_MODEL), row(512), row(512), row(512), row(512), row(512), row(D_MODEL), row(D_MODEL),
                  _const_spec((8, RW_WIDTH)), _const_spec((RW_WIDTH, RW_WIDTH)),
                  _const_spec((512, D_MODEL)), _const_spec((512, D_MODEL)), _const_spec((D_MODEL, D_MODEL))],
        out_specs=row(D_MODEL),
        out_shape=jax.ShapeDtypeStruct((n, D_MODEL), F32),
        compiler_params=pltpu.CompilerParams(dimension_semantics=("parallel",), vmem_limit_bytes=VMEM_LIMIT),
        name="out_stage",
    )(x2d, oa, y, bon, ga, gb, ma, mb, lw["v512"], lw["seg"], lw["w_out_a"], lw["w_out_b"], lw["w_o"])


def _swap_halves(w):
    half = ROPE // 2
    return jnp.concatenate([w[..., half:], w[..., :half]], axis=-1)


def _prep_weights(norm_w, w_in, q_norm_w, kv_norm_w, w_uq, w_ukv, qn_nope, qn_rope, kn_nope, kn_rope,
                  mu_shift, w0, w2, a0, a2, k_k, k_a, r_k, lnx_w, lnx_b, w_out_a, w_out_b, w_o):
    depth = w_in.shape[0]
    z = lambda *s: jnp.zeros((depth,) + s, F32)
    kpe = w_in[..., 384:416]
    w_lat = jnp.concatenate([w_in[..., :384], kpe, z(D_MODEL, 96), _swap_halves(kpe), z(D_MODEL, 96)],
                            axis=-1).astype(BF16)
    w_in_p = w_in[..., 416:].astype(BF16)
    uq = w_uq.reshape(depth, Q_RANK, HEADS, QK)
    uq_p = jnp.concatenate([uq, z(Q_RANK, HEADS, 32)], axis=-1).reshape(depth, Q_RANK, HEADS * LANES)
    uq_s = jnp.concatenate([z(Q_RANK, HEADS, NOPE), _swap_halves(uq[..., NOPE:]), z(Q_RANK, HEADS, 32)],
                           axis=-1).reshape(depth, Q_RANK, HEADS * LANES)
    ukv = w_ukv.reshape(depth, KV_RANK, HEADS, 2 * NOPE)
    zk = z(KV_RANK, HEADS, NOPE)
    k_part = jnp.concatenate([ukv[..., :NOPE], zk], axis=-1)
    vv = ukv[..., NOPE:]
    odd = (jnp.arange(HEADS) % 2 == 1)[None, None, :, None]
    v_part = jnp.where(odd, jnp.concatenate([zk, vv], axis=-1), jnp.concatenate([vv, zk], axis=-1))
    ukv_p = jnp.swapaxes(jnp.concatenate([k_part, v_part], axis=-1), 1, 2).astype(BF16)
    zg = lambda n: jnp.zeros((depth, n), F32)
    g128 = jnp.stack([
        jnp.concatenate([qn_nope, qn_rope, zg(32)], axis=-1),
        jnp.concatenate([zg(NOPE), _swap_halves(qn_rope), zg(32)], axis=-1),
        jnp.concatenate([kn_rope, zg(96)], axis=-1),
        jnp.concatenate([_swap_halves(kn_rope), zg(96)], axis=-1),
        kv_norm_w,
        jnp.concatenate([kn_nope, zg(NOPE)], axis=-1),
        zg(LANES), zg(LANES)], axis=1)
    v512 = jnp.stack([w0, a0, k_k, k_a, r_k.reshape(depth, RW_WIDTH), lnx_w, lnx_b, zg(RW_WIDTH)], axis=1)
    seg_id = jnp.arange(RW_WIDTH) // RW_HEAD
    seg = (seg_id[:, None] == seg_id[None, :]).astype(BF16)
    onecol = jnp.stack([jnp.arange(LANES) == NOPE, jnp.arange(LANES) == 0]).astype(F32)
    place = (jnp.arange(ROPE)[:, None] + NOPE == jnp.arange(LANES)[None, :]).astype(BF16)
    return dict(
        norm_w=norm_w[:, None, :], w_lat=w_lat, w_in=w_in_p, q_norm_w=q_norm_w[:, None, :],
        w_uq=uq_p.astype(BF16), w_uq_sw=uq_s.astype(BF16), w_ukv=ukv_p, g128=g128, mu=mu_shift[:, None, :],
        v512=v512, w2=jnp.concatenate([w2, z(LORA, RW_WIDTH)], axis=1),
        a2=jnp.concatenate([z(LORA, RW_WIDTH), a2], axis=1),
        w_out_a=w_out_a.astype(BF16), w_out_b=w_out_b.astype(BF16), w_o=w_o.astype(BF16),
    ), dict(seg=seg, onecol=onecol, place=place)


def _rope_table(past, seq, batch):
    half = ROPE // 2
    inv = ROPE_THETA ** (-jnp.arange(half, dtype=F32) / half)
    pos = past + jnp.arange(seq, dtype=jnp.int32)
    ang = inv[:, None] * pos.astype(F32)[None, :]
    cs = jnp.concatenate([jnp.cos(ang), jnp.sin(ang)], axis=0)
    sel = jnp.zeros((2 * half, 4 * LANES), F32)
    j = jnp.arange(half)
    for tile, base, row0 in ((0, NOPE, 0), (1, NOPE, half), (2, 0, 0), (3, 0, half)):
        sign_lo = 1.0 if row0 == 0 else -1.0
        sel = sel.at[row0 + j, tile * LANES + base + j].set(sign_lo)
        sel = sel.at[row0 + j, tile * LANES + base + half + j].set(1.0)
    bias = jnp.zeros((1, 4 * LANES), F32).at[0, :NOPE].set(1.0)
    tab = lax.dot_general(cs, sel, _TN, precision=HI, preferred_element_type=F32) + bias
    return jnp.tile(tab, (batch, 1))


def _state_to_pairs(s):
    b = s.shape[0]
    m = jnp.swapaxes(s.astype(F32), -1, -2).reshape(b, HEAD_PAIRS, 2, RW_HEAD, RW_HEAD)
    zero = jnp.zeros_like(m[:, :, 0])
    top = jnp.concatenate([m[:, :, 0], zero], axis=-1)
    bot = jnp.concatenate([zero, m[:, :, 1]], axis=-1)
    return jnp.concatenate([top, bot], axis=-2)


def _pairs_to_state(m):
    b = m.shape[0]
    e = m[:, :, :RW_HEAD, :RW_HEAD]
    o = m[:, :, RW_HEAD:, RW_HEAD:]
    return jnp.swapaxes(jnp.stack([e, o], axis=2).reshape(b, HEADS, RW_HEAD, RW_HEAD), -1, -2)


def _group_layer(x2d, tab, lw, ckv_past2d, kpe_past2d, layer, m0, sh0, tri, *, batch, seq, past):
    q, ckv, kpe, rw, bon, ga, gb, ma, mb, shift = _in_stage(x2d, tab, sh0, lw, seq=seq)
    ckv3 = ckv.reshape(batch, seq, KV_RANK)
    kpe3 = kpe.reshape(batch, seq, ROPE)
    if past > 0:
        oa = _decode_stage(q, ckv_past2d, kpe_past2d, layer * batch, ckv, kpe, lw, batch=batch, seq=seq, past=past)
    else:
        tq = tk = min(512, seq)
        assert seq % tk == 0
        kv_tile = max(t for t in (128, 256, 512, 1024) if seq % t == 0)
        k_att, v_att = _kv_stage(ckv, kpe, lw, tile=kv_tile)
        oa = _attn_stage(q, k_att, v_att, batch=batch, seq=seq, past=0, kv_len=seq, kv_pad=seq, tq=tq, tk=tk)
    y, m1 = _wkv_stage(rw, m0, tri, batch=batch, seq=seq)
    out = _out_stage(x2d, oa, y, bon, ga, gb, ma, mb, lw)
    return out, ckv3, kpe3, _pairs_to_state(m1), shift.reshape(batch, 1, SHIFT_WIDTH)


def kernel(x_prompt, x_sample, cache_ckv, cache_kpe, state_wkv, state_shift, norm_w, w_in, q_norm_w, kv_norm_w,
           w_uq, w_ukv, qn_nope, qn_rope, kn_nope, kn_rope, mu_shift, w0, w2, a0, a2, k_k, k_a, r_k, lnx_w, lnx_b,
           w_out_a, w_out_b, w_o):
    depth = w_in.shape[0]
    bp, tp, _ = x_prompt.shape
    bs, ts, _ = x_sample.shape
    past = cache_ckv.shape[2]
    assert past % CHUNK == 0 and tp % CHUNK == 0 and ts % CHUNK == 0
    stacked, shared = _prep_weights(norm_w, w_in, q_norm_w, kv_norm_w, w_uq, w_ukv, qn_nope, qn_rope, kn_nope,
                                    kn_rope, mu_shift, w0, w2, a0, a2, k_k, k_a, r_k, lnx_w, lnx_b,
                                    w_out_a, w_out_b, w_o)
    tab_p = _rope_table(0, tp, bp)
    tab_s = _rope_table(past, ts, bs)
    tri = (jnp.arange(CHUNK)[:, None] >= jnp.arange(CHUNK)[None, :]).astype(BF16)
    zero_m = jnp.zeros((bp, HEAD_PAIRS, LANES, LANES), F32)
    zero_sh = jnp.zeros((bp, 1, SHIFT_WIDTH), F32)

    ckv_past2d = cache_ckv.reshape(depth * bs * past, KV_RANK)
    kpe_past2d = cache_kpe.reshape(depth * bs * past, ROPE)
    yp = x_prompt.reshape(bp * tp, D_MODEL)
    ys = x_sample.reshape(bs * ts, D_MODEL)
    outs_p, outs_s = [], []
    for i in range(depth):
        lw = {name: val[i] for name, val in stacked.items()}
        lw.update(shared)
        yp, *rest_p = _group_layer(yp, tab_p, lw, None, None, i, zero_m, zero_sh, tri, batch=bp, seq=tp, past=0)
        outs_p.append(rest_p)
        ys, *rest_s = _group_layer(ys, tab_s, lw, ckv_past2d, kpe_past2d, i, _state_to_pairs(state_wkv[i]),
                                   state_shift[i], tri, batch=bs, seq=ts, past=past)
        outs_s.append(rest_s)
    stack = lambda outs, j: jnp.stack([o[j] for o in outs])
    return (yp.reshape(bp, tp, D_MODEL), ys.reshape(bs, ts, D_MODEL),
            stack(outs_p, 0), stack(outs_p, 1), stack(outs_p, 2), stack(outs_p, 3),
            stack(outs_s, 0), stack(outs_s, 1), stack(outs_s, 2), stack(outs_s, 3))
```

```python
import functools

import jax
import jax.numpy as jnp
from jax import lax
from jax.experimental import pallas as pl
from jax.experimental.pallas import tpu as pltpu

F32 = jnp.float32
BF16 = jnp.bfloat16
HI = lax.Precision.HIGHEST

D_MODEL = 1024
CHUNK = 64
CHUNK_SHIFT = 6
EPS = 1e-6
GN_EPS = 64e-5
HEADS = 8
NOPE = 64
ROPE = 32
QK = NOPE + ROPE
Q_RANK = 256
KV_RANK = 128
ROPE_THETA = 10000.0
RW_WIDTH = 512
RW_HEAD = 64
LORA = 64
SHIFT_WIDTH = 3 * RW_WIDTH + 2 * LORA
LANES = 128
HEAD_PAIRS = HEADS // 2
NEG = -(2.0 ** 100)
P_PROD, P_INV, P_W, P_U, P_Y, P_M = 1, 1, 1, 1, 1, 1
EXP_M05 = 0.6065306597126334
LOG2_E = 1.4426950408889634

W_LAT = 640
C_GA = 0
C_ZS = C_GA + 512
C_GB = C_ZS + SHIFT_WIDTH
C_MA = C_GB + 512
C_MB = C_MA + D_MODEL
C_END = C_MB + D_MODEL

VMEM_LIMIT = 56 * 1024 * 1024


def _const_spec(shape):
    n = len(shape)
    return pl.BlockSpec(shape, lambda *_: (0,) * n)


_NN = (((1,), (0,)), ((), ()))
_NT = (((1,), (1,)), ((), ()))
_TN = (((0,), (0,)), ((), ()))


def _dot(a, b):
    return jnp.dot(a, b, preferred_element_type=F32)


def _dot_nt(a, b):
    return lax.dot_general(a, b, _NT, preferred_element_type=F32)


def _split(x):
    hi = x.astype(BF16)
    return hi, (x - hi.astype(F32)).astype(BF16)


def _mx(a, b, passes, dims=_NN):
    if passes == 6:
        return lax.dot_general(a, b, dims, precision=HI, preferred_element_type=F32)
    dg = lambda x, y: lax.dot_general(x, y, dims, preferred_element_type=F32)
    if passes == 1:
        return dg(a.astype(BF16), b.astype(BF16))
    ah, al = _split(a)
    bh, bl = _split(b)
    return dg(ah, bh) + (dg(ah, bl) + dg(al, bh))


def _seg_sum(x, seg, passes=2):
    hi = x.astype(BF16)
    if passes == 1:
        return _dot(hi, seg)
    lo = (x - hi.astype(F32)).astype(BF16)
    return _dot(hi, seg) + _dot(lo, seg)


def _in_kernel(x_ref, tab_ref, sh0_ref, normw_ref, wlat_ref, win_ref, qnw_ref, wuq_ref, wuqs_ref, g128_ref, mu_ref,
               v512_ref, w2_ref, a2_ref, seg_ref,
               q_ref, ckv_ref, kpe_ref, rw_ref, bon_ref, ga_ref, gb_ref, ma_ref, mb_ref, shift_ref,
               carry_ref, *, tm, seq, scale):
    i = pl.program_id(0)
    x = x_ref[...]
    h = x * lax.rsqrt(jnp.mean(x * x, axis=-1, keepdims=True) + EPS) * normw_ref[...]
    hb = h.astype(BF16)
    tab = tab_ref[...]
    g128 = g128_ref[...]

    z1 = _dot(hb, wlat_ref[...])
    cq = z1[:, :Q_RANK]
    cqn = cq * lax.rsqrt(jnp.mean(cq * cq, axis=-1, keepdims=True) + EPS) * qnw_ref[...]
    kvl = z1[:, Q_RANK:Q_RANK + KV_RANK]
    ckv_ref[...] = kvl * lax.rsqrt(jnp.mean(kvl * kvl, axis=-1, keepdims=True) + EPS) * g128[4:5]
    t1 = z1[:, 384:512]
    t2 = z1[:, 512:640]
    kinv = lax.rsqrt(jnp.sum(t1 * t1, axis=-1, keepdims=True) * (1.0 / ROPE) + EPS)
    kr = (t1 * g128[2:3] * tab[:, 256:384] + t2 * g128[3:4] * tab[:, 384:512]) * kinv
    kpe_ref[...] = kr[:, :ROPE]

    cqb = cqn.astype(BF16)
    qr = _dot(cqb, wuq_ref[...])
    qs = _dot(cqb, wuqs_ref[...])
    lane = lax.broadcasted_iota(jnp.int32, (1, LANES), 1)
    is_nope = lane < NOPE
    cos_f = tab[:, 0:128]
    sin_f = tab[:, 128:256]
    for hd in range(HEADS):
        t = qr[:, hd * LANES:(hd + 1) * LANES]
        sq = t * t
        ms_n = jnp.sum(jnp.where(is_nope, sq, 0.0), axis=-1, keepdims=True) * (1.0 / NOPE)
        ms_r = jnp.sum(jnp.where(is_nope, 0.0, sq), axis=-1, keepdims=True) * (1.0 / ROPE)
        inv = jnp.where(is_nope, lax.rsqrt(ms_n + EPS), lax.rsqrt(ms_r + EPS))
        qh = (t * g128[0:1] * cos_f + qs[:, hd * LANES:(hd + 1) * LANES] * g128[1:2] * sin_f) * (inv * scale)
        q_ref[hd] = qh.astype(BF16)

    ga_ref[...] = jax.nn.silu(_dot(hb, win_ref[:, C_GA:C_ZS])).astype(BF16)
    gb_ref[...] = jax.nn.silu(_dot(hb, win_ref[:, C_GB:C_MA])).astype(BF16)
    ma_ref[...] = jax.nn.sigmoid(_dot(hb, win_ref[:, C_MA:C_MB])).astype(BF16)
    mb_ref[...] = jax.nn.sigmoid(_dot(hb, win_ref[:, C_MB:C_END])).astype(BF16)

    zs = _dot(hb, win_ref[:, C_ZS:C_GB])
    rows = lax.broadcasted_iota(jnp.int32, (tm, 1), 0)
    prev = pltpu.roll(zs, 1, axis=0)
    if seq >= tm:
        tiles_per_seq = seq // tm

        @pl.when(i == 0)
        def _init_carry():
            carry_ref[...] = jnp.zeros(carry_ref.shape, F32)

        first = jnp.where(i % tiles_per_seq == 0, sh0_ref[0], carry_ref[...])
        prev = jnp.where(rows == 0, first, prev)
        carry_ref[...] = zs[tm - 1:tm, :]
        shift_ref[0] = zs[tm - 1:tm, :]
    else:
        for s in range(tm // seq):
            prev = jnp.where(rows == s * seq, sh0_ref[0, s:s + 1, :], prev)
            shift_ref[0, s:s + 1, :] = zs[(s + 1) * seq - 1:(s + 1) * seq, :]
    zm = zs + (prev - zs) * mu_ref[...]
    r = zm[:, 0:512]
    k = zm[:, 512:1024]
    v = zm[:, 1024:1536]
    t13 = zm[:, 1536:1664]
    v512 = v512_ref[...]
    w0, a0, k_k, k_a, r_k = v512[0:1], v512[1:2], v512[2:3], v512[3:4], v512[4:5]
    seg = seg_ref[...]
    lw = -EXP_M05 * jax.nn.sigmoid(w0 + _mx(jnp.tanh(t13), w2_ref[...], 3))
    aic = jax.nn.sigmoid(a0 + _mx(t13, a2_ref[...], 3))
    kkr = k * k_k
    kk = kkr * lax.rsqrt(jnp.maximum(_seg_sum(kkr * kkr, seg, 1), 1e-24))
    k2 = k * (1.0 + (aic - 1.0) * k_a)
    bon_ref[...] = (_seg_sum(r * k2 * r_k, seg, 1) * v).astype(BF16)
    rw_ref[:, 0:512] = r
    rw_ref[:, 512:1024] = lw
    rw_ref[:, 1024:1536] = k2
    rw_ref[:, 1536:2048] = v
    rw_ref[:, 2048:2560] = -kk
    rw_ref[:, 2560:3072] = kk * aic


def _in_stage(x2d, tab, sh0, lw, *, seq):
    n = x2d.shape[0]
    tm = 256 if seq >= 256 else min(256, n)
    assert n % tm == 0 and (seq % tm == 0 or tm % seq == 0)
    nseq = max(1, tm // seq)
    batch = n // seq
    sh0 = sh0.reshape(batch // nseq, nseq, SHIFT_WIDTH)
    if seq >= tm:
        sh_map = lambda i: (i // (seq // tm), 0, 0)
    else:
        sh_map = lambda i: (i, 0, 0)
    row = lambda w: pl.BlockSpec((tm, w), lambda i: (i, 0))
    out_shape = (
        jax.ShapeDtypeStruct((HEADS, n, LANES), BF16),
        jax.ShapeDtypeStruct((n, KV_RANK), F32),
        jax.ShapeDtypeStruct((n, ROPE), F32),
        jax.ShapeDtypeStruct((n, 6 * RW_WIDTH), F32),
        jax.ShapeDtypeStruct((n, RW_WIDTH), BF16),
        jax.ShapeDtypeStruct((n, 512), BF16),
        jax.ShapeDtypeStruct((n, 512), BF16),
        jax.ShapeDtypeStruct((n, D_MODEL), BF16),
        jax.ShapeDtypeStruct((n, D_MODEL), BF16),
        jax.ShapeDtypeStruct((batch // nseq, nseq, SHIFT_WIDTH), F32),
    )
    out_specs = (
        pl.BlockSpec((HEADS, tm, LANES), lambda i: (0, i, 0)),
        row(KV_RANK), row(ROPE), row(6 * RW_WIDTH), row(RW_WIDTH), row(512), row(512), row(D_MODEL), row(D_MODEL),
        pl.BlockSpec((1, nseq, SHIFT_WIDTH), sh_map),
    )
    in_specs = [
        row(D_MODEL), row(512), pl.BlockSpec((1, nseq, SHIFT_WIDTH), sh_map),
        _const_spec((1, D_MODEL)), _const_spec((D_MODEL, W_LAT)),
        pl.BlockSpec((D_MODEL, C_END), lambda i: (0, 0), pipeline_mode=pl.Buffered(1)),
        _const_spec((1, Q_RANK)), _const_spec((Q_RANK, HEADS * LANES)), _const_spec((Q_RANK, HEADS * LANES)),
        _const_spec((8, LANES)), _const_spec((1, SHIFT_WIDTH)), _const_spec((8, RW_WIDTH)),
        _const_spec((LANES, RW_WIDTH)), _const_spec((LANES, RW_WIDTH)), _const_spec((RW_WIDTH, RW_WIDTH)),
    ]
    return pl.pallas_call(
        functools.partial(_in_kernel, tm=tm, seq=seq, scale=QK ** -0.5 * LOG2_E),
        grid=(n // tm,),
        in_specs=in_specs,
        out_specs=out_specs,
        out_shape=out_shape,
        scratch_shapes=[pltpu.VMEM((1, SHIFT_WIDTH), F32)],
        compiler_params=pltpu.CompilerParams(dimension_semantics=("arbitrary",), vmem_limit_bytes=VMEM_LIMIT),
        name="in_stage",
    )(x2d, tab, sh0, lw["norm_w"], lw["w_lat"], lw["w_in"], lw["q_norm_w"], lw["w_uq"], lw["w_uq_sw"], lw["g128"], lw["mu"],
      lw["v512"], lw["w2"], lw["a2"], lw["seg"])


def _head_keys_values(kv, kpe, one):
    kr = kv[:, :LANES]
    inv = lax.rsqrt(jnp.sum(kr * kr, axis=-1, keepdims=True) * (1.0 / NOPE) + EPS)
    return (kr * inv + kpe).astype(BF16), (kv[:, LANES:] + one).astype(BF16)


def _kv_kernel(ckv_ref, kpe_ref, wukv_ref, one_ref, place_ref, k_ref, v_ref):
    ckv = ckv_ref[...].astype(BF16)
    kpe = _dot(kpe_ref[...].astype(BF16), place_ref[...])
    for hd in range(HEADS):
        k_ref[hd], v_ref[hd] = _head_keys_values(_dot(ckv, wukv_ref[hd]), kpe, one_ref[hd % 2:hd % 2 + 1])


def _kv_stage(ckv2d, kpe2d, lw, *, tile):
    n = ckv2d.shape[0]
    assert n % tile == 0
    hspec = pl.BlockSpec((HEADS, tile, LANES), lambda i: (0, i, 0))
    return pl.pallas_call(
        _kv_kernel,
        grid=(n // tile,),
        in_specs=[pl.BlockSpec((tile, KV_RANK), lambda i: (i, 0)), pl.BlockSpec((tile, ROPE), lambda i: (i, 0)),
                  _const_spec((HEADS, KV_RANK, 2 * LANES)), _const_spec((2, LANES)), _const_spec((ROPE, LANES))],
        out_specs=(hspec, hspec),
        out_shape=(jax.ShapeDtypeStruct((HEADS, n, LANES), BF16),) * 2,
        compiler_params=pltpu.CompilerParams(dimension_semantics=("parallel",), vmem_limit_bytes=VMEM_LIMIT),
        name="kv_stage",
    )(ckv2d, kpe2d, lw["w_ukv"], lw["onecol"], lw["place"])


def _decode_kernel(q_ref, ckvp_ref, kpep_ref, ckvn_ref, kpen_ref, wukv_ref, one_ref, place_ref, o_ref):
    place = place_ref[...]
    ckv_p = ckvp_ref[...].astype(BF16)
    ckv_n = ckvn_ref[...].astype(BF16)
    kpe_p = lax.dot_general(kpep_ref[...].astype(BF16), place, _TN, preferred_element_type=F32)
    kpe_n = _dot(kpen_ref[...].astype(BF16), place)

    def keys_values(hd):
        one = one_ref[hd % 2:hd % 2 + 1]
        w = wukv_ref[hd]
        return (_head_keys_values(_dot(ckv_p, w), kpe_p, one) +
                _head_keys_values(_dot(ckv_n, w), kpe_n, one))

    acc = []
    nxt = keys_values(0)
    for hd in range(HEADS):
        kp, vp, kn, vn = nxt
        q = q_ref[hd]
        sp = _dot_nt(q, kp).astype(BF16)
        sn = _dot_nt(q, kn).astype(BF16)
        if hd + 1 < HEADS:
            nxt = keys_values(hd + 1)
        m = jnp.maximum(jnp.max(sp, axis=-1, keepdims=True), jnp.max(sn, axis=-1, keepdims=True))
        acc.append(_dot(jnp.exp2(sp - m), vp) + _dot(jnp.exp2(sn - m), vn))
    lane = lax.broadcasted_iota(jnp.int32, (1, LANES), 1)
    for p in range(HEAD_PAIRS):
        ae, ao = acc[2 * p], acc[2 * p + 1]
        o_ref[:, p * LANES:(p + 1) * LANES] = jnp.where(lane < 64, ae / ae[:, 64:65], ao / ao[:, 0:1])


def _decode_stage(q, ckv_past2d, kpe_past2d, past_block, ckv_new, kpe_new, lw, *, batch, seq, past):
    assert seq == CHUNK and past % CHUNK == 0
    return pl.pallas_call(
        _decode_kernel,
        grid=(batch,),
        in_specs=[pl.BlockSpec((HEADS, seq, LANES), lambda b: (0, b, 0)),
                  pl.BlockSpec((past, KV_RANK), lambda b: (past_block + b, 0)),
                  pl.BlockSpec((ROPE, past), lambda b: (past_block + b, 0)),
                  pl.BlockSpec((seq, KV_RANK), lambda b: (b, 0)),
                  pl.BlockSpec((seq, ROPE), lambda b: (b, 0)),
                  _const_spec((HEADS, KV_RANK, 2 * LANES)), _const_spec((2, LANES)), _const_spec((ROPE, LANES))],
        out_specs=pl.BlockSpec((seq, HEADS * NOPE), lambda b: (b, 0)),
        out_shape=jax.ShapeDtypeStruct((batch * seq, HEADS * NOPE), F32),
        compiler_params=pltpu.CompilerParams(dimension_semantics=("parallel",), vmem_limit_bytes=VMEM_LIMIT),
        name="decode_attention",
    )(q, ckv_past2d, kpe_past2d, ckv_new, kpe_new, lw["w_ukv"], lw["onecol"], lw["place"])


def _last_key_tile(qi, *, tq, tk, past):
    return (past + qi * tq + tq - 1) // tk


def _attn_kernel(qi_ref, kj_ref, q_ref, k_ref, v_ref, o_ref, m_scr, acc_scr, *, tq, tk, past, kv_len, nk):
    t = pl.program_id(1)
    qi = qi_ref[t]
    kj = kj_ref[t]

    @pl.when(kj == 0)
    def _init():
        m_scr[...] = jnp.full(m_scr.shape, NEG, F32)
        acc_scr[...] = jnp.zeros(acc_scr.shape, F32)

    q_first = past + qi * tq
    k_first = kj * tk
    k_last = k_first + tk - 1
    unmasked = jnp.logical_and((k_last >> CHUNK_SHIFT) <= (q_first >> CHUNK_SHIFT), k_last < kv_len)

    def compute(masked):
        if masked:
            rows = q_first + lax.broadcasted_iota(jnp.int32, (tq, tk), 0)
            cols = k_first + lax.broadcasted_iota(jnp.int32, (tq, tk), 1)
            ok = jnp.logical_and((cols >> CHUNK_SHIFT) <= (rows >> CHUNK_SHIFT), cols < kv_len)
        s_next = _dot_nt(q_ref[0], k_ref[0])
        for hd in range(HEADS):
            s = s_next
            if hd + 1 < HEADS:
                s_next = _dot_nt(q_ref[hd + 1], k_ref[hd + 1])
            s = s.astype(BF16)
            if masked:
                s = jnp.where(ok, s, jnp.asarray(NEG, BF16))
            m_prev = m_scr[hd]
            m_new = jnp.maximum(m_prev, jnp.max(s, axis=-1, keepdims=True).astype(F32))
            alpha = jnp.exp2(m_prev - m_new)
            p = jnp.exp2(s - jnp.concatenate([m_new.astype(BF16)] * (tk // LANES), axis=1))
            acc_scr[hd] = acc_scr[hd] * alpha + _dot(p, v_ref[hd])
            m_scr[hd] = m_new

    @pl.when(unmasked)
    def _full():
        compute(False)

    @pl.when(jnp.logical_not(unmasked))
    def _diag():
        compute(True)

    @pl.when(kj == jnp.minimum(_last_key_tile(qi, tq=tq, tk=tk, past=past), nk - 1))
    def _finish():
        lane = lax.broadcasted_iota(jnp.int32, (1, LANES), 1)
        for p in range(HEAD_PAIRS):
            ae = acc_scr[2 * p]
            ao = acc_scr[2 * p + 1]
            o_ref[:, p * LANES:(p + 1) * LANES] = jnp.where(lane < 64, ae / ae[:, 64:65], ao / ao[:, 0:1])


def _attn_stage(q, k, v, *, batch, seq, past, kv_len, kv_pad, tq, tk):
    nq = seq // tq
    nk = kv_pad // tk
    assert seq % tq == 0 and kv_pad % tk == 0 and tq % CHUNK == 0 and past % CHUNK == 0
    steps = [(qi, kj) for qi in range(nq)
             for kj in range(min(_last_key_tile(qi, tq=tq, tk=tk, past=past), nk - 1) + 1)]
    qi_tab = jnp.asarray([s[0] for s in steps], jnp.int32)
    kj_tab = jnp.asarray([s[1] for s in steps], jnp.int32)
    kv_map = lambda b, t, qi, kj: (0, b * nk + kj[t], 0)
    grid_spec = pltpu.PrefetchScalarGridSpec(
        num_scalar_prefetch=2,
        grid=(batch, len(steps)),
        in_specs=[pl.BlockSpec((HEADS, tq, LANES), lambda b, t, qi, kj: (0, b * nq + qi[t], 0)),
                  pl.BlockSpec((HEADS, tk, LANES), kv_map),
                  pl.BlockSpec((HEADS, tk, LANES), kv_map)],
        out_specs=pl.BlockSpec((tq, HEADS * NOPE), lambda b, t, qi, kj: (b * nq + qi[t], 0)),
        scratch_shapes=[pltpu.VMEM((HEADS, tq, LANES), F32), pltpu.VMEM((HEADS, tq, LANES), F32)],
    )
    return pl.pallas_call(
        functools.partial(_attn_kernel, tq=tq, tk=tk, past=past, kv_len=kv_len, nk=nk),
        grid_spec=grid_spec,
        out_shape=jax.ShapeDtypeStruct((batch * seq, HEADS * NOPE), F32),
        compiler_params=pltpu.CompilerParams(dimension_semantics=("parallel", "arbitrary"),
                                             vmem_limit_bytes=VMEM_LIMIT),
        name="attention",
    )(qi_tab, kj_tab, q, k, v)


def _wkv_kernel(rw_ref, m0_ref, tri_ref, y_ref, mout_ref, m_scr, *, nsteps, nch):
    j = pl.program_id(1)

    @pl.when(j == 0)
    def _load():
        m_scr[...] = m0_ref[0]

    L = CHUNK
    lane = lax.broadcasted_iota(jnp.int32, (L, LANES), 1)
    even = lane < RW_HEAD
    ri = lax.broadcasted_iota(jnp.int32, (LANES, LANES), 0)
    ci = lax.broadcasted_iota(jnp.int32, (LANES, LANES), 1)
    tr = ri & (L - 1)
    tc = ci & (L - 1)
    strict = tr > tc
    incl = tr >= tc
    blk16 = (ri >> 4) == (ci >> 4)
    ident = jnp.where(ri == ci, 1.0, 0.0).astype(F32)

    def stack(t):
        return jnp.concatenate([jnp.where(even, t, 0.0), jnp.where(even, 0.0, t)], axis=0)

    pairs = range(HEAD_PAIRS)
    each = lambda fn, *cols: [fn(*args) for args in zip(*cols)]
    cat0 = lambda *xs: jnp.concatenate(xs, axis=0)
    cat1 = lambda *xs: jnp.concatenate(xs, axis=1)
    sls = [slice(p * LANES, (p + 1) * LANES) for p in pairs]

    a_s, r_s, b_s, k_s, bh_s, kh_s, v_s, gcol = [], [], [], [], [], [], [], []
    tri = tri_ref[...]
    for c in range(nch):
        rows = slice(c * L, (c + 1) * L)
        r = rw_ref[rows, 0:512]
        lw = rw_ref[rows, 512:1024]
        k = rw_ref[rows, 1024:1536]
        v = rw_ref[rows, 1536:2048]
        a = rw_ref[rows, 2048:2560]
        b = rw_ref[rows, 2560:3072]
        lw_hi, lw_rest = lw.astype(BF16), lw - lw.astype(BF16).astype(F32)
        lw_mid = lw_rest.astype(BF16)
        lw_lo = (lw_rest - lw_mid.astype(F32)).astype(BF16)
        cum = _dot(tri, lw_hi) + (_dot(tri, lw_mid) + _dot(tri, lw_lo))
        gt = cum[L - 1:L, :]
        en = jnp.exp(-cum)
        eh = jnp.exp(gt - cum)
        egt = jnp.exp(gt)
        for dst, val in ((a_s, a * jnp.exp(cum - lw)), (r_s, r * jnp.exp(cum)), (b_s, b * en), (k_s, k * en),
                         (bh_s, b * eh), (kh_s, k * eh), (v_s, v)):
            dst.extend(stack(val[:, sl]) for sl in sls)
        gcol.extend(jnp.transpose(jnp.broadcast_to(egt[:, sl], (LANES, LANES))) for sl in sls)
    prod = each(lambda a_, r_, b_, k_: _mx(cat0(a_, r_), cat0(b_, k_), P_PROD, _NT), a_s, r_s, b_s, k_s)
    n_ab = [jnp.where(strict, x[0:LANES, 0:LANES], 0.0) for x in prod]
    a_ak = [jnp.where(strict, x[0:LANES, LANES:2 * LANES], 0.0) for x in prod]
    a_rb = [jnp.where(incl, x[LANES:2 * LANES, 0:LANES], 0.0) for x in prod]
    a_rk = [jnp.where(incl, x[LANES:2 * LANES, LANES:2 * LANES], 0.0) for x in prod]
    inv = lambda x, y_: _mx(x, y_, P_INV)
    d1 = [jnp.where(blk16, x, 0.0) for x in n_ab]
    e1 = each(lambda x, d: x - d, n_ab, d1)
    p1 = [ident + d for d in d1]
    d2 = each(inv, d1, d1)
    p2 = each(lambda p_, d: p_ + inv(p_, d), p1, d2)
    d4 = each(inv, d2, d2)
    p3 = each(lambda p_, d: p_ + inv(p_, d), p2, d4)
    d8 = each(inv, d4, d4)
    t_d = each(lambda p_, d: p_ + inv(p_, d), p3, d8)
    f1 = each(inv, t_d, e1)
    f2 = each(inv, f1, f1)
    if1 = [ident + f for f in f1]
    g1 = each(lambda i_, f: i_ + inv(i_, f), if1, f2)
    t_m = each(inv, g1, t_d)

    m = [m_scr[p] for p in pairs]
    for c in range(nch):
        ch = slice(c * HEAD_PAIRS, (c + 1) * HEAD_PAIRS)
        w = each(lambda a_, ak, m_, v_: _mx(cat1(a_, ak), cat0(m_, v_), P_W), a_s[ch], a_ak[ch], m, v_s[ch])
        u = each(lambda t_, w_: _mx(t_, w_, P_U), t_m[ch], w)
        y = each(lambda r_, rb, rk, m_, u_, v_: _mx(cat1(r_, rb, rk), cat0(m_, u_, v_), P_Y),
                 r_s[ch], a_rb[ch], a_rk[ch], m, u, v_s[ch])
        m = each(lambda b_, k_, u_, v_, m_, g_: m_ * g_ + _mx(cat0(b_, k_), cat0(u_, v_), P_M, _TN),
                 bh_s[ch], kh_s[ch], u, v_s[ch], m, gcol[ch])
        for p in pairs:
            y_ref[c * L:(c + 1) * L, sls[p]] = y[p][0:L] + y[p][L:2 * L]
    for p in pairs:
        m_scr[p] = m[p]

    @pl.when(j == nsteps - 1)
    def _store():
        mout_ref[0] = m_scr[...]


def _wkv_stage(rw, m0, tri, *, batch, seq):
    assert seq % CHUNK == 0
    nch = max(d for d in (4, 2, 1) if (seq // CHUNK) % d == 0)
    rows = nch * CHUNK
    nsteps = seq // rows
    return pl.pallas_call(
        functools.partial(_wkv_kernel, nsteps=nsteps, nch=nch),
        grid=(batch, nsteps),
        in_specs=[pl.BlockSpec((rows, 6 * RW_WIDTH), lambda b, j: (b * nsteps + j, 0)),
                  pl.BlockSpec((1, HEAD_PAIRS, LANES, LANES), lambda b, j: (b, 0, 0, 0)),
                  _const_spec((CHUNK, CHUNK))],
        out_specs=(pl.BlockSpec((rows, RW_WIDTH), lambda b, j: (b * nsteps + j, 0)),
                   pl.BlockSpec((1, HEAD_PAIRS, LANES, LANES), lambda b, j: (b, 0, 0, 0))),
        out_shape=(jax.ShapeDtypeStruct((batch * seq, RW_WIDTH), F32),
                   jax.ShapeDtypeStruct((batch, HEAD_PAIRS, LANES, LANES), F32)),
        scratch_shapes=[pltpu.VMEM((HEAD_PAIRS, LANES, LANES), F32)],
        compiler_params=pltpu.CompilerParams(dimension_semantics=("parallel", "arbitrary"),
                                             vmem_limit_bytes=VMEM_LIMIT),
        name="wkv",
    )(rw, m0, tri)


def _out_kernel(x_ref, oa_ref, y_ref, bon_ref, ga_ref, gb_ref, ma_ref, mb_ref, v512_ref, seg_ref,
                woa_ref, wob_ref, wo_ref, out_ref):
    v512 = v512_ref[...]
    lnx_w, lnx_b = v512[5:6], v512[6:7]
    seg = seg_ref[...]
    oa = (oa_ref[...] * ga_ref[...]).astype(BF16)
    y = y_ref[...]
    d = y - _seg_sum(y, seg) * (1.0 / RW_HEAD)
    var = _seg_sum(d * d, seg) * (1.0 / RW_HEAD)
    yn = d * lax.rsqrt(var + GN_EPS) * lnx_w + lnx_b
    ob = ((yn + bon_ref[...]) * gb_ref[...]).astype(BF16)
    m = ma_ref[...] * _dot(oa, woa_ref[...]) + mb_ref[...] * _dot(ob, wob_ref[...])
    out_ref[...] = x_ref[...] + _dot(m.astype(BF16), wo_ref[...])


def _out_stage(x2d, oa, y, bon, ga, gb, ma, mb, lw):
    n = x2d.shape[0]
    tm = min(256, n)
    assert n % tm == 0
    row = lambda w: pl.BlockSpec((tm, w), lambda i: (i, 0))
    return pl.pallas_call(
        _out_kernel,
        grid=(n // tm,),
        in_specs=[row(D_MODEL), row(512), row(512), row(512), row(512), row(512), row(D_MODEL), row(D_MODEL),
                  _const_spec((8, RW_WIDTH)), _const_spec((RW_WIDTH, RW_WIDTH)),
                  _const_spec((512, D_MODEL)), _const_spec((512, D_MODEL)), _const_spec((D_MODEL, D_MODEL))],
        out_specs=row(D_MODEL),
        out_shape=jax.ShapeDtypeStruct((n, D_MODEL), F32),
        compiler_params=pltpu.CompilerParams(dimension_semantics=("parallel",), vmem_limit_bytes=VMEM_LIMIT),
        name="out_stage",
    )(x2d, oa, y, bon, ga, gb, ma, mb, lw["v512"], lw["seg"], lw["w_out_a"], lw["w_out_b"], lw["w_o"])


def _swap_halves(w):
    half = ROPE // 2
    return jnp.concatenate([w[..., half:], w[..., :half]], axis=-1)


def _prep_weights(norm_w, w_in, q_norm_w, kv_norm_w, w_uq, w_ukv, qn_nope, qn_rope, kn_nope, kn_rope,
                  mu_shift, w0, w2, a0, a2, k_k, k_a, r_k, lnx_w, lnx_b, w_out_a, w_out_b, w_o):
    depth = w_in.shape[0]
    z = lambda *s: jnp.zeros((depth,) + s, F32)
    kpe = w_in[..., 384:416]
    w_lat = jnp.concatenate([w_in[..., :384], kpe, z(D_MODEL, 96), _swap_halves(kpe), z(D_MODEL, 96)],
                            axis=-1).astype(BF16)
    w_in_p = w_in[..., 416:].astype(BF16)
    uq = w_uq.reshape(depth, Q_RANK, HEADS, QK)
    uq_p = jnp.concatenate([uq, z(Q_RANK, HEADS, 32)], axis=-1).reshape(depth, Q_RANK, HEADS * LANES)
    uq_s = jnp.concatenate([z(Q_RANK, HEADS, NOPE), _swap_halves(uq[..., NOPE:]), z(Q_RANK, HEADS, 32)],
                           axis=-1).reshape(depth, Q_RANK, HEADS * LANES)
    ukv = w_ukv.reshape(depth, KV_RANK, HEADS, 2 * NOPE)
    zk = z(KV_RANK, HEADS, NOPE)
    k_part = jnp.concatenate([ukv[..., :NOPE], zk], axis=-1)
    vv = ukv[..., NOPE:]
    odd = (jnp.arange(HEADS) % 2 == 1)[None, None, :, None]
    v_part = jnp.where(odd, jnp.concatenate([zk, vv], axis=-1), jnp.concatenate([vv, zk], axis=-1))
    ukv_p = jnp.swapaxes(jnp.concatenate([k_part, v_part], axis=-1), 1, 2).astype(BF16)
    zg = lambda n: jnp.zeros((depth, n), F32)
    g128 = jnp.stack([
        jnp.concatenate([qn_nope * kn_nope, qn_rope, zg(32)], axis=-1),
        jnp.concatenate([zg(NOPE), _swap_halves(qn_rope), zg(32)], axis=-1),
        jnp.concatenate([kn_rope, zg(96)], axis=-1),
        jnp.concatenate([_swap_halves(kn_rope), zg(96)], axis=-1),
        kv_norm_w,
        zg(LANES), zg(LANES), zg(LANES)], axis=1)
    v512 = jnp.stack([w0, a0, k_k, k_a, r_k.reshape(depth, RW_WIDTH), lnx_w, lnx_b, zg(RW_WIDTH)], axis=1)
    seg_id = jnp.arange(RW_WIDTH) // RW_HEAD
    seg = (seg_id[:, None] == seg_id[None, :]).astype(BF16)
    onecol = jnp.stack([jnp.arange(LANES) == NOPE, jnp.arange(LANES) == 0]).astype(F32)
    place = (jnp.arange(ROPE)[:, None] + NOPE == jnp.arange(LANES)[None, :]).astype(BF16)
    return dict(
        norm_w=norm_w[:, None, :], w_lat=w_lat, w_in=w_in_p, q_norm_w=q_norm_w[:, None, :],
        w_uq=uq_p.astype(BF16), w_uq_sw=uq_s.astype(BF16), w_ukv=ukv_p, g128=g128, mu=mu_shift[:, None, :],
        v512=v512, w2=jnp.concatenate([w2, z(LORA, RW_WIDTH)], axis=1),
        a2=jnp.concatenate([z(LORA, RW_WIDTH), a2], axis=1),
        w_out_a=w_out_a.astype(BF16), w_out_b=w_out_b.astype(BF16), w_o=w_o.astype(BF16),
    ), dict(seg=seg, onecol=onecol, place=place)


def _rope_table(past, seq, batch):
    half = ROPE // 2
    inv = ROPE_THETA ** (-jnp.arange(half, dtype=F32) / half)
    pos = past + jnp.arange(seq, dtype=jnp.int32)
    ang = inv[:, None] * pos.astype(F32)[None, :]
    cs = jnp.concatenate([jnp.cos(ang), jnp.sin(ang)], axis=0)
    sel = jnp.zeros((2 * half, 4 * LANES), F32)
    j = jnp.arange(half)
    for tile, base, row0 in ((0, NOPE, 0), (1, NOPE, half), (2, 0, 0), (3, 0, half)):
        sign_lo = 1.0 if row0 == 0 else -1.0
        sel = sel.at[row0 + j, tile * LANES + base + j].set(sign_lo)
        sel = sel.at[row0 + j, tile * LANES + base + half + j].set(1.0)
    bias = jnp.zeros((1, 4 * LANES), F32).at[0, :NOPE].set(1.0)
    tab = lax.dot_general(cs, sel, _TN, precision=HI, preferred_element_type=F32) + bias
    return jnp.tile(tab, (batch, 1))


def _state_to_pairs(s):
    b = s.shape[0]
    m = jnp.swapaxes(s.astype(F32), -1, -2).reshape(b, HEAD_PAIRS, 2, RW_HEAD, RW_HEAD)
    zero = jnp.zeros_like(m[:, :, 0])
    top = jnp.concatenate([m[:, :, 0], zero], axis=-1)
    bot = jnp.concatenate([zero, m[:, :, 1]], axis=-1)
    return jnp.concatenate([top, bot], axis=-2)


def _pairs_to_state(m):
    b = m.shape[0]
    e = m[:, :, :RW_HEAD, :RW_HEAD]
    o = m[:, :, RW_HEAD:, RW_HEAD:]
    return jnp.swapaxes(jnp.stack([e, o], axis=2).reshape(b, HEADS, RW_HEAD, RW_HEAD), -1, -2)


def _group_layer(x2d, tab, lw, ckv_past2d, kpe_past2d, layer, m0, sh0, tri, *, batch, seq, past):
    q, ckv, kpe, rw, bon, ga, gb, ma, mb, shift = _in_stage(x2d, tab, sh0, lw, seq=seq)
    ckv3 = ckv.reshape(batch, seq, KV_RANK)
    kpe3 = kpe.reshape(batch, seq, ROPE)
    if past > 0:
        oa = _decode_stage(q, ckv_past2d, kpe_past2d, layer * batch, ckv, kpe, lw, batch=batch, seq=seq, past=past)
    else:
        tq = tk = min(512, seq)
        assert seq % tk == 0
        kv_tile = max(t for t in (128, 256, 512, 1024) if seq % t == 0)
        k_att, v_att = _kv_stage(ckv, kpe, lw, tile=kv_tile)
        oa = _attn_stage(q, k_att, v_att, batch=batch, seq=seq, past=0, kv_len=seq, kv_pad=seq, tq=tq, tk=tk)
    y, m1 = _wkv_stage(rw, m0, tri, batch=batch, seq=seq)
    out = _out_stage(x2d, oa, y, bon, ga, gb, ma, mb, lw)
    return out, ckv3, kpe3, _pairs_to_state(m1), shift.reshape(batch, 1, SHIFT_WIDTH)


def kernel(x_prompt, x_sample, cache_ckv, cache_kpe, state_wkv, state_shift, norm_w, w_in, q_norm_w, kv_norm_w,
           w_uq, w_ukv, qn_nope, qn_rope, kn_nope, kn_rope, mu_shift, w0, w2, a0, a2, k_k, k_a, r_k, lnx_w, lnx_b,
           w_out_a, w_out_b, w_o):
    depth = w_in.shape[0]
    bp, tp, _ = x_prompt.shape
    bs, ts, _ = x_sample.shape
    past = cache_ckv.shape[2]
    assert past % CHUNK == 0 and tp % CHUNK == 0 and ts % CHUNK == 0
    stacked, shared = _prep_weights(norm_w, w_in, q_norm_w, kv_norm_w, w_uq, w_ukv, qn_nope, qn_rope, kn_nope,
                                    kn_rope, mu_shift, w0, w2, a0, a2, k_k, k_a, r_k, lnx_w, lnx_b,
                                    w_out_a, w_out_b, w_o)
    tab_p = _rope_table(0, tp, bp)
    tab_s = _rope_table(past, ts, bs)
    tri = (jnp.arange(CHUNK)[:, None] >= jnp.arange(CHUNK)[None, :]).astype(BF16)
    zero_m = jnp.zeros((bp, HEAD_PAIRS, LANES, LANES), F32)
    zero_sh = jnp.zeros((bp, 1, SHIFT_WIDTH), F32)

    ckv_past2d = cache_ckv.reshape(depth * bs * past, KV_RANK)
    kpe_past2d = jnp.swapaxes(cache_kpe, 2, 3).reshape(depth * bs * ROPE, past)
    yp = x_prompt.reshape(bp * tp, D_MODEL)
    ys = x_sample.reshape(bs * ts, D_MODEL)
    outs_p, outs_s = [], []
    for i in range(depth):
        lw = {name: val[i] for name, val in stacked.items()}
        lw.update(shared)
        yp, *rest_p = _group_layer(yp, tab_p, lw, None, None, i, zero_m, zero_sh, tri, batch=bp, seq=tp, past=0)
        outs_p.append(rest_p)
        ys, *rest_s = _group_layer(ys, tab_s, lw, ckv_past2d, kpe_past2d, i, _state_to_pairs(state_wkv[i]),
                                   state_shift[i], tri, batch=bs, seq=ts, past=past)
        outs_s.append(rest_s)
    stack = lambda outs, j: jnp.stack([o[j] for o in outs])
    return (yp.reshape(bp, tp, D_MODEL), ys.reshape(bs, ts, D_MODEL),
            stack(outs_p, 0), stack(outs_p, 1), stack(outs_p, 2), stack(outs_p, 3),
            stack(outs_s, 0), stack(outs_s, 1), stack(outs_s, 2), stack(outs_s, 3))
```

```python
import functools

import jax
import jax.numpy as jnp
from jax import lax
from jax.experimental import pallas as pl
from jax.experimental.pallas import tpu as pltpu

F32 = jnp.float32
BF16 = jnp.bfloat16
HI = lax.Precision.HIGHEST

D_MODEL = 1024
CHUNK = 64
CHUNK_SHIFT = 6
EPS = 1e-6
GN_EPS = 64e-5
HEADS = 8
NOPE = 64
ROPE = 32
QK = NOPE + ROPE
Q_RANK = 256
KV_RANK = 128
ROPE_THETA = 10000.0
RW_WIDTH = 512
RW_HEAD = 64
LORA = 64
SHIFT_WIDTH = 3 * RW_WIDTH + 2 * LORA
LANES = 128
HEAD_PAIRS = HEADS // 2
NEG = -(2.0 ** 100)
P_PROD, P_INV, P_W, P_U, P_Y, P_M = 1, 1, 1, 1, 1, 1
EXP_M05 = 0.6065306597126334
LOG2_E = 1.4426950408889634

W_LAT = 640
C_GA = 0
C_ZS = C_GA + 512
C_GB = C_ZS + SHIFT_WIDTH
C_MA = C_GB + 512
C_MB = C_MA + D_MODEL
C_END = C_MB + D_MODEL

VMEM_LIMIT = 56 * 1024 * 1024


def _const_spec(shape):
    n = len(shape)
    return pl.BlockSpec(shape, lambda *_: (0,) * n)


_NN = (((1,), (0,)), ((), ()))
_NT = (((1,), (1,)), ((), ()))
_TN = (((0,), (0,)), ((), ()))


def _dot(a, b):
    return jnp.dot(a, b, preferred_element_type=F32)


def _dot_nt(a, b):
    return lax.dot_general(a, b, _NT, preferred_element_type=F32)


def _split(x):
    hi = x.astype(BF16)
    return hi, (x - hi.astype(F32)).astype(BF16)


def _mx(a, b, passes, dims=_NN):
    if passes == 6:
        return lax.dot_general(a, b, dims, precision=HI, preferred_element_type=F32)
    dg = lambda x, y: lax.dot_general(x, y, dims, preferred_element_type=F32)
    if passes == 1:
        return dg(a.astype(BF16), b.astype(BF16))
    ah, al = _split(a)
    bh, bl = _split(b)
    return dg(ah, bh) + (dg(ah, bl) + dg(al, bh))


def _seg_sum(x, seg, passes=2):
    hi = x.astype(BF16)
    if passes == 1:
        return _dot(hi, seg)
    lo = (x - hi.astype(F32)).astype(BF16)
    return _dot(hi, seg) + _dot(lo, seg)


def _in_kernel(x_ref, tab_ref, sh0_ref, normw_ref, wlat_ref, win_ref, qnw_ref, wuq_ref, wuqs_ref, g128_ref, mu_ref,
               v512_ref, w2_ref, a2_ref, seg_ref,
               q_ref, ckv_ref, kpe_ref, rw_ref, bon_ref, ga_ref, gb_ref, ma_ref, mb_ref, shift_ref,
               carry_ref, *, tm, seq, scale):
    i = pl.program_id(0)
    x = x_ref[...]
    h = x * lax.rsqrt(jnp.mean(x * x, axis=-1, keepdims=True) + EPS) * normw_ref[...]
    hb = h.astype(BF16)
    tab = tab_ref[...]
    g128 = g128_ref[...]

    z1 = _dot(hb, wlat_ref[...])
    cq = z1[:, :Q_RANK]
    cqn = cq * lax.rsqrt(jnp.mean(cq * cq, axis=-1, keepdims=True) + EPS) * qnw_ref[...]
    kvl = z1[:, Q_RANK:Q_RANK + KV_RANK]
    ckv_ref[...] = kvl * lax.rsqrt(jnp.mean(kvl * kvl, axis=-1, keepdims=True) + EPS) * g128[4:5]
    t1 = z1[:, 384:512]
    t2 = z1[:, 512:640]
    kinv = lax.rsqrt(jnp.sum(t1 * t1, axis=-1, keepdims=True) * (1.0 / ROPE) + EPS)
    kr = (t1 * g128[2:3] * tab[:, 256:384] + t2 * g128[3:4] * tab[:, 384:512]) * kinv
    kpe_ref[...] = kr[:, :ROPE]

    cqb = cqn.astype(BF16)
    qr = _dot(cqb, wuq_ref[...])
    qs = _dot(cqb, wuqs_ref[...])
    lane = lax.broadcasted_iota(jnp.int32, (1, LANES), 1)
    is_nope = lane < NOPE
    cos_f = tab[:, 0:128]
    sin_f = tab[:, 128:256]
    for hd in range(HEADS):
        t = qr[:, hd * LANES:(hd + 1) * LANES]
        sq = t * t
        ms_n = jnp.sum(jnp.where(is_nope, sq, 0.0), axis=-1, keepdims=True) * (1.0 / NOPE)
        ms_r = jnp.sum(jnp.where(is_nope, 0.0, sq), axis=-1, keepdims=True) * (1.0 / ROPE)
        inv = jnp.where(is_nope, lax.rsqrt(ms_n + EPS), lax.rsqrt(ms_r + EPS))
        qh = (t * g128[0:1] * cos_f + qs[:, hd * LANES:(hd + 1) * LANES] * g128[1:2] * sin_f) * (inv * scale)
        q_ref[hd] = qh.astype(BF16)

    ga_ref[...] = jax.nn.silu(_dot(hb, win_ref[:, C_GA:C_ZS])).astype(BF16)
    gb_ref[...] = jax.nn.silu(_dot(hb, win_ref[:, C_GB:C_MA])).astype(BF16)
    ma_ref[...] = jax.nn.sigmoid(_dot(hb, win_ref[:, C_MA:C_MB])).astype(BF16)
    mb_ref[...] = jax.nn.sigmoid(_dot(hb, win_ref[:, C_MB:C_END])).astype(BF16)

    zs = _dot(hb, win_ref[:, C_ZS:C_GB])
    rows = lax.broadcasted_iota(jnp.int32, (tm, 1), 0)
    prev = pltpu.roll(zs, 1, axis=0)
    if seq >= tm:
        tiles_per_seq = seq // tm

        @pl.when(i == 0)
        def _init_carry():
            carry_ref[...] = jnp.zeros(carry_ref.shape, F32)

        first = jnp.where(i % tiles_per_seq == 0, sh0_ref[0], carry_ref[...])
        prev = jnp.where(rows == 0, first, prev)
        carry_ref[...] = zs[tm - 1:tm, :]
        shift_ref[0] = zs[tm - 1:tm, :]
    else:
        for s in range(tm // seq):
            prev = jnp.where(rows == s * seq, sh0_ref[0, s:s + 1, :], prev)
            shift_ref[0, s:s + 1, :] = zs[(s + 1) * seq - 1:(s + 1) * seq, :]
    zm = zs + (prev - zs) * mu_ref[...]
    r = zm[:, 0:512]
    k = zm[:, 512:1024]
    v = zm[:, 1024:1536]
    t13 = zm[:, 1536:1664]
    v512 = v512_ref[...]
    w0, a0, k_k, k_a, r_k = v512[0:1], v512[1:2], v512[2:3], v512[3:4], v512[4:5]
    seg = seg_ref[...]
    lw = -EXP_M05 * jax.nn.sigmoid(w0 + _mx(jnp.tanh(t13), w2_ref[...], 3))
    aic = jax.nn.sigmoid(a0 + _mx(t13, a2_ref[...], 3))
    kkr = k * k_k
    kk = kkr * lax.rsqrt(jnp.maximum(_seg_sum(kkr * kkr, seg, 1), 1e-24))
    k2 = k * (1.0 + (aic - 1.0) * k_a)
    bon_ref[...] = (_seg_sum(r * k2 * r_k, seg, 1) * v).astype(BF16)
    rw_ref[:, 0:512] = r
    rw_ref[:, 512:1024] = lw
    rw_ref[:, 1024:1536] = k2
    rw_ref[:, 1536:2048] = v
    rw_ref[:, 2048:2560] = -kk
    rw_ref[:, 2560:3072] = kk * aic


def _in_stage(x2d, tab, sh0, lw, *, seq):
    n = x2d.shape[0]
    tm = 256 if seq >= 256 else min(256, n)
    assert n % tm == 0 and (seq % tm == 0 or tm % seq == 0)
    nseq = max(1, tm // seq)
    batch = n // seq
    sh0 = sh0.reshape(batch // nseq, nseq, SHIFT_WIDTH)
    if seq >= tm:
        sh_map = lambda i: (i // (seq // tm), 0, 0)
    else:
        sh_map = lambda i: (i, 0, 0)
    row = lambda w: pl.BlockSpec((tm, w), lambda i: (i, 0))
    out_shape = (
        jax.ShapeDtypeStruct((HEADS, n, LANES), BF16),
        jax.ShapeDtypeStruct((n, KV_RANK), F32),
        jax.ShapeDtypeStruct((n, ROPE), F32),
        jax.ShapeDtypeStruct((n, 6 * RW_WIDTH), F32),
        jax.ShapeDtypeStruct((n, RW_WIDTH), BF16),
        jax.ShapeDtypeStruct((n, 512), BF16),
        jax.ShapeDtypeStruct((n, 512), BF16),
        jax.ShapeDtypeStruct((n, D_MODEL), BF16),
        jax.ShapeDtypeStruct((n, D_MODEL), BF16),
        jax.ShapeDtypeStruct((batch // nseq, nseq, SHIFT_WIDTH), F32),
    )
    out_specs = (
        pl.BlockSpec((HEADS, tm, LANES), lambda i: (0, i, 0)),
        row(KV_RANK), row(ROPE), row(6 * RW_WIDTH), row(RW_WIDTH), row(512), row(512), row(D_MODEL), row(D_MODEL),
        pl.BlockSpec((1, nseq, SHIFT_WIDTH), sh_map),
    )
    in_specs = [
        row(D_MODEL), row(512), pl.BlockSpec((1, nseq, SHIFT_WIDTH), sh_map),
        _const_spec((1, D_MODEL)), _const_spec((D_MODEL, W_LAT)),
        pl.BlockSpec((D_MODEL, C_END), lambda i: (0, 0), pipeline_mode=pl.Buffered(1)),
        _const_spec((1, Q_RANK)), _const_spec((Q_RANK, HEADS * LANES)), _const_spec((Q_RANK, HEADS * LANES)),
        _const_spec((8, LANES)), _const_spec((1, SHIFT_WIDTH)), _const_spec((8, RW_WIDTH)),
        _const_spec((LANES, RW_WIDTH)), _const_spec((LANES, RW_WIDTH)), _const_spec((RW_WIDTH, RW_WIDTH)),
    ]
    return pl.pallas_call(
        functools.partial(_in_kernel, tm=tm, seq=seq, scale=QK ** -0.5 * LOG2_E),
        grid=(n // tm,),
        in_specs=in_specs,
        out_specs=out_specs,
        out_shape=out_shape,
        scratch_shapes=[pltpu.VMEM((1, SHIFT_WIDTH), F32)],
        compiler_params=pltpu.CompilerParams(dimension_semantics=("arbitrary",), vmem_limit_bytes=VMEM_LIMIT),
        name="in_stage",
    )(x2d, tab, sh0, lw["norm_w"], lw["w_lat"], lw["w_in"], lw["q_norm_w"], lw["w_uq"], lw["w_uq_sw"], lw["g128"], lw["mu"],
      lw["v512"], lw["w2"], lw["a2"], lw["seg"])


def _head_keys_values(kv, kpe, one):
    kr = kv[:, :LANES]
    inv = lax.rsqrt(jnp.sum(kr * kr, axis=-1, keepdims=True) * (1.0 / NOPE) + EPS)
    return (kr * inv + kpe).astype(BF16), (kv[:, LANES:] + one).astype(BF16)


def _kv_kernel(ckv_ref, kpe_ref, wukv_ref, one_ref, place_ref, k_ref, v_ref):
    ckv = ckv_ref[...].astype(BF16)
    kpe = _dot(kpe_ref[...].astype(BF16), place_ref[...])
    for hd in range(HEADS):
        k_ref[hd], v_ref[hd] = _head_keys_values(_dot(ckv, wukv_ref[hd]), kpe, one_ref[hd % 2:hd % 2 + 1])


def _kv_stage(ckv2d, kpe2d, lw, *, tile):
    n = ckv2d.shape[0]
    assert n % tile == 0
    hspec = pl.BlockSpec((HEADS, tile, LANES), lambda i: (0, i, 0))
    return pl.pallas_call(
        _kv_kernel,
        grid=(n // tile,),
        in_specs=[pl.BlockSpec((tile, KV_RANK), lambda i: (i, 0)), pl.BlockSpec((tile, ROPE), lambda i: (i, 0)),
                  _const_spec((HEADS, KV_RANK, 2 * LANES)), _const_spec((2, LANES)), _const_spec((ROPE, LANES))],
        out_specs=(hspec, hspec),
        out_shape=(jax.ShapeDtypeStruct((HEADS, n, LANES), BF16),) * 2,
        compiler_params=pltpu.CompilerParams(dimension_semantics=("parallel",), vmem_limit_bytes=VMEM_LIMIT),
        name="kv_stage",
    )(ckv2d, kpe2d, lw["w_ukv"], lw["onecol"], lw["place"])


def _decode_kernel(q_ref, ckvp_ref, kpep_ref, ckvn_ref, kpen_ref, wukv_ref, one_ref, place_ref, o_ref):
    place = place_ref[...]
    ckv_p = ckvp_ref[...].astype(BF16)
    ckv_n = ckvn_ref[...].astype(BF16)
    kpe_p = lax.dot_general(kpep_ref[...].astype(BF16), place, _TN, preferred_element_type=F32)
    kpe_n = _dot(kpen_ref[...].astype(BF16), place)

    def keys_values(hd):
        one = one_ref[hd % 2:hd % 2 + 1]
        w = wukv_ref[hd]
        return (_head_keys_values(_dot(ckv_p, w), kpe_p, one) +
                _head_keys_values(_dot(ckv_n, w), kpe_n, one))

    acc = []
    ahead = [keys_values(0), keys_values(1)]
    for hd in range(HEADS):
        kp, vp, kn, vn = ahead.pop(0)
        q = q_ref[hd]
        sp = _dot_nt(q, kp).astype(BF16)
        sn = _dot_nt(q, kn).astype(BF16)
        if hd + 2 < HEADS:
            ahead.append(keys_values(hd + 2))
        m = jnp.maximum(jnp.max(sp, axis=-1, keepdims=True), jnp.max(sn, axis=-1, keepdims=True))
        acc.append(_dot(jnp.exp2(sp - m), vp) + _dot(jnp.exp2(sn - m), vn))
    lane = lax.broadcasted_iota(jnp.int32, (1, LANES), 1)
    for p in range(HEAD_PAIRS):
        ae, ao = acc[2 * p], acc[2 * p + 1]
        o_ref[:, p * LANES:(p + 1) * LANES] = jnp.where(lane < 64, ae / ae[:, 64:65], ao / ao[:, 0:1])


def _decode_stage(q, ckv_past2d, kpe_past2d, past_block, ckv_new, kpe_new, lw, *, batch, seq, past):
    assert seq == CHUNK and past % CHUNK == 0
    return pl.pallas_call(
        _decode_kernel,
        grid=(batch,),
        in_specs=[pl.BlockSpec((HEADS, seq, LANES), lambda b: (0, b, 0)),
                  pl.BlockSpec((past, KV_RANK), lambda b: (past_block + b, 0)),
                  pl.BlockSpec((ROPE, past), lambda b: (past_block + b, 0)),
                  pl.BlockSpec((seq, KV_RANK), lambda b: (b, 0)),
                  pl.BlockSpec((seq, ROPE), lambda b: (b, 0)),
                  _const_spec((HEADS, KV_RANK, 2 * LANES)), _const_spec((2, LANES)), _const_spec((ROPE, LANES))],
        out_specs=pl.BlockSpec((seq, HEADS * NOPE), lambda b: (b, 0)),
        out_shape=jax.ShapeDtypeStruct((batch * seq, HEADS * NOPE), F32),
        compiler_params=pltpu.CompilerParams(dimension_semantics=("parallel",), vmem_limit_bytes=VMEM_LIMIT),
        name="decode_attention",
    )(q, ckv_past2d, kpe_past2d, ckv_new, kpe_new, lw["w_ukv"], lw["onecol"], lw["place"])


def _last_key_tile(qi, *, tq, tk, past):
    return (past + qi * tq + tq - 1) // tk


def _attn_kernel(qi_ref, kj_ref, q_ref, k_ref, v_ref, o_ref, m_scr, acc_scr, *, tq, tk, past, kv_len, nk):
    t = pl.program_id(1)
    qi = qi_ref[t]
    kj = kj_ref[t]

    @pl.when(kj == 0)
    def _init():
        m_scr[...] = jnp.full(m_scr.shape, NEG, F32)
        acc_scr[...] = jnp.zeros(acc_scr.shape, F32)

    q_first = past + qi * tq
    k_first = kj * tk
    k_last = k_first + tk - 1
    unmasked = jnp.logical_and((k_last >> CHUNK_SHIFT) <= (q_first >> CHUNK_SHIFT), k_last < kv_len)

    def compute(masked):
        if masked:
            rows = q_first + lax.broadcasted_iota(jnp.int32, (tq, tk), 0)
            cols = k_first + lax.broadcasted_iota(jnp.int32, (tq, tk), 1)
            ok = jnp.logical_and((cols >> CHUNK_SHIFT) <= (rows >> CHUNK_SHIFT), cols < kv_len)
        s_next = _dot_nt(q_ref[0], k_ref[0])
        for hd in range(HEADS):
            s = s_next
            if hd + 1 < HEADS:
                s_next = _dot_nt(q_ref[hd + 1], k_ref[hd + 1])
            s = s.astype(BF16)
            if masked:
                s = jnp.where(ok, s, jnp.asarray(NEG, BF16))
            m_prev = m_scr[hd]
            m_new = jnp.maximum(m_prev, jnp.max(s, axis=-1, keepdims=True).astype(F32))
            alpha = jnp.exp2(m_prev - m_new)
            p = jnp.exp2(s - jnp.concatenate([m_new.astype(BF16)] * (tk // LANES), axis=1))
            acc_scr[hd] = acc_scr[hd] * alpha + _dot(p, v_ref[hd])
            m_scr[hd] = m_new

    @pl.when(unmasked)
    def _full():
        compute(False)

    @pl.when(jnp.logical_not(unmasked))
    def _diag():
        compute(True)

    @pl.when(kj == jnp.minimum(_last_key_tile(qi, tq=tq, tk=tk, past=past), nk - 1))
    def _finish():
        lane = lax.broadcasted_iota(jnp.int32, (1, LANES), 1)
        for p in range(HEAD_PAIRS):
            ae = acc_scr[2 * p]
            ao = acc_scr[2 * p + 1]
            o_ref[:, p * LANES:(p + 1) * LANES] = jnp.where(lane < 64, ae / ae[:, 64:65], ao / ao[:, 0:1])


def _attn_stage(q, k, v, *, batch, seq, past, kv_len, kv_pad, tq, tk):
    nq = seq // tq
    nk = kv_pad // tk
    assert seq % tq == 0 and kv_pad % tk == 0 and tq % CHUNK == 0 and past % CHUNK == 0
    steps = [(qi, kj) for qi in range(nq)
             for kj in range(min(_last_key_tile(qi, tq=tq, tk=tk, past=past), nk - 1) + 1)]
    qi_tab = jnp.asarray([s[0] for s in steps], jnp.int32)
    kj_tab = jnp.asarray([s[1] for s in steps], jnp.int32)
    kv_map = lambda b, t, qi, kj: (0, b * nk + kj[t], 0)
    grid_spec = pltpu.PrefetchScalarGridSpec(
        num_scalar_prefetch=2,
        grid=(batch, len(steps)),
        in_specs=[pl.BlockSpec((HEADS, tq, LANES), lambda b, t, qi, kj: (0, b * nq + qi[t], 0)),
                  pl.BlockSpec((HEADS, tk, LANES), kv_map),
                  pl.BlockSpec((HEADS, tk, LANES), kv_map)],
        out_specs=pl.BlockSpec((tq, HEADS * NOPE), lambda b, t, qi, kj: (b * nq + qi[t], 0)),
        scratch_shapes=[pltpu.VMEM((HEADS, tq, LANES), F32), pltpu.VMEM((HEADS, tq, LANES), F32)],
    )
    return pl.pallas_call(
        functools.partial(_attn_kernel, tq=tq, tk=tk, past=past, kv_len=kv_len, nk=nk),
        grid_spec=grid_spec,
        out_shape=jax.ShapeDtypeStruct((batch * seq, HEADS * NOPE), F32),
        compiler_params=pltpu.CompilerParams(dimension_semantics=("parallel", "arbitrary"),
                                             vmem_limit_bytes=VMEM_LIMIT),
        name="attention",
    )(qi_tab, kj_tab, q, k, v)


def _wkv_kernel(rw_ref, m0_ref, tri_ref, y_ref, mout_ref, m_scr, *, nsteps, nch):
    j = pl.program_id(1)

    @pl.when(j == 0)
    def _load():
        m_scr[...] = m0_ref[0]

    L = CHUNK
    lane = lax.broadcasted_iota(jnp.int32, (L, LANES), 1)
    even = lane < RW_HEAD
    ri = lax.broadcasted_iota(jnp.int32, (LANES, LANES), 0)
    ci = lax.broadcasted_iota(jnp.int32, (LANES, LANES), 1)
    tr = ri & (L - 1)
    tc = ci & (L - 1)
    strict = tr > tc
    incl = tr >= tc
    blk16 = (ri >> 4) == (ci >> 4)
    ident = jnp.where(ri == ci, 1.0, 0.0).astype(F32)

    def stack(t):
        return jnp.concatenate([jnp.where(even, t, 0.0), jnp.where(even, 0.0, t)], axis=0).astype(BF16)

    pairs = range(HEAD_PAIRS)
    each = lambda fn, *cols: [fn(*args) for args in zip(*cols)]
    cat0 = lambda *xs: jnp.concatenate(xs, axis=0)
    cat1 = lambda *xs: jnp.concatenate(xs, axis=1)
    sls = [slice(p * LANES, (p + 1) * LANES) for p in pairs]

    a_s, r_s, b_s, k_s, bh_s, kh_s, v_s, gcol = [], [], [], [], [], [], [], []
    tri = tri_ref[...]
    for c in range(nch):
        rows = slice(c * L, (c + 1) * L)
        r = rw_ref[rows, 0:512]
        lw = rw_ref[rows, 512:1024]
        k = rw_ref[rows, 1024:1536]
        v = rw_ref[rows, 1536:2048]
        a = rw_ref[rows, 2048:2560]
        b = rw_ref[rows, 2560:3072]
        lw_hi, lw_rest = lw.astype(BF16), lw - lw.astype(BF16).astype(F32)
        lw_mid = lw_rest.astype(BF16)
        lw_lo = (lw_rest - lw_mid.astype(F32)).astype(BF16)
        cum = _dot(tri, lw_hi) + (_dot(tri, lw_mid) + _dot(tri, lw_lo))
        gt = cum[L - 1:L, :]
        en = jnp.exp(-cum)
        eh = jnp.exp(gt - cum)
        egt = jnp.exp(gt)
        for dst, val in ((a_s, a * jnp.exp(cum - lw)), (r_s, r * jnp.exp(cum)), (b_s, b * en), (k_s, k * en),
                         (bh_s, b * eh), (kh_s, k * eh), (v_s, v)):
            dst.extend(stack(val[:, sl]) for sl in sls)
        gcol.extend(jnp.transpose(jnp.broadcast_to(egt[:, sl], (LANES, LANES))) for sl in sls)
    prod = each(lambda a_, r_, b_, k_: _mx(cat0(a_, r_), cat0(b_, k_), P_PROD, _NT), a_s, r_s, b_s, k_s)
    n_ab = [jnp.where(strict, x[0:LANES, 0:LANES], 0.0) for x in prod]
    a_ak = [jnp.where(strict, x[0:LANES, LANES:2 * LANES], 0.0) for x in prod]
    a_rb = [jnp.where(incl, x[LANES:2 * LANES, 0:LANES], 0.0) for x in prod]
    a_rk = [jnp.where(incl, x[LANES:2 * LANES, LANES:2 * LANES], 0.0) for x in prod]
    inv = lambda x, y_: _mx(x, y_, P_INV)
    d1 = [jnp.where(blk16, x, 0.0) for x in n_ab]
    e1 = each(lambda x, d: x - d, n_ab, d1)
    p1 = [ident + d for d in d1]
    d2 = each(inv, d1, d1)
    p2 = each(lambda p_, d: p_ + inv(p_, d), p1, d2)
    d4 = each(inv, d2, d2)
    p3 = each(lambda p_, d: p_ + inv(p_, d), p2, d4)
    d8 = each(inv, d4, d4)
    t_d = each(lambda p_, d: p_ + inv(p_, d), p3, d8)
    f1 = each(inv, t_d, e1)
    f2 = each(inv, f1, f1)
    if1 = [ident + f for f in f1]
    g1 = each(lambda i_, f: i_ + inv(i_, f), if1, f2)
    t_m = each(inv, g1, t_d)
    t_ak = each(lambda t_, a_, ak: _mx(t_, cat1(a_, ak), P_W), t_m, a_s, a_ak)

    m = [m_scr[p] for p in pairs]
    for c in range(nch):
        ch = slice(c * HEAD_PAIRS, (c + 1) * HEAD_PAIRS)
        u = each(lambda ta_, m_, v_: _mx(ta_, cat0(m_, v_), P_U), t_ak[ch], m, v_s[ch])
        y = each(lambda r_, rb, rk, m_, u_, v_: _mx(cat1(r_, rb, rk), cat0(m_, u_, v_), P_Y),
                 r_s[ch], a_rb[ch], a_rk[ch], m, u, v_s[ch])
        m = each(lambda b_, k_, u_, v_, m_, g_: m_ * g_ + _mx(cat0(b_, k_), cat0(u_, v_), P_M, _TN),
                 bh_s[ch], kh_s[ch], u, v_s[ch], m, gcol[ch])
        for p in pairs:
            y_ref[c * L:(c + 1) * L, sls[p]] = y[p][0:L] + y[p][L:2 * L]
    for p in pairs:
        m_scr[p] = m[p]

    @pl.when(j == nsteps - 1)
    def _store():
        mout_ref[0] = m_scr[...]


def _wkv_stage(rw, m0, tri, *, batch, seq):
    assert seq % CHUNK == 0
    nch = max(d for d in (4, 2, 1) if (seq // CHUNK) % d == 0)
    rows = nch * CHUNK
    nsteps = seq // rows
    return pl.pallas_call(
        functools.partial(_wkv_kernel, nsteps=nsteps, nch=nch),
        grid=(batch, nsteps),
        in_specs=[pl.BlockSpec((rows, 6 * RW_WIDTH), lambda b, j: (b * nsteps + j, 0)),
                  pl.BlockSpec((1, HEAD_PAIRS, LANES, LANES), lambda b, j: (b, 0, 0, 0)),
                  _const_spec((CHUNK, CHUNK))],
        out_specs=(pl.BlockSpec((rows, RW_WIDTH), lambda b, j: (b * nsteps + j, 0)),
                   pl.BlockSpec((1, HEAD_PAIRS, LANES, LANES), lambda b, j: (b, 0, 0, 0))),
        out_shape=(jax.ShapeDtypeStruct((batch * seq, RW_WIDTH), F32),
                   jax.ShapeDtypeStruct((batch, HEAD_PAIRS, LANES, LANES), F32)),
        scratch_shapes=[pltpu.VMEM((HEAD_PAIRS, LANES, LANES), F32)],
        compiler_params=pltpu.CompilerParams(dimension_semantics=("parallel", "arbitrary"),
                                             vmem_limit_bytes=VMEM_LIMIT),
        name="wkv",
    )(rw, m0, tri)


def _out_kernel(x_ref, oa_ref, y_ref, bon_ref, ga_ref, gb_ref, ma_ref, mb_ref, v512_ref, seg_ref,
                woa_ref, wob_ref, wo_ref, out_ref):
    v512 = v512_ref[...]
    lnx_w, lnx_b = v512[5:6], v512[6:7]
    seg = seg_ref[...]
    oa = (oa_ref[...] * ga_ref[...]).astype(BF16)
    y = y_ref[...]
    d = y - _seg_sum(y, seg) * (1.0 / RW_HEAD)
    var = _seg_sum(d * d, seg) * (1.0 / RW_HEAD)
    yn = d * lax.rsqrt(var + GN_EPS) * lnx_w + lnx_b
    ob = ((yn + bon_ref[...]) * gb_ref[...]).astype(BF16)
    m = ma_ref[...] * _dot(oa, woa_ref[...]) + mb_ref[...] * _dot(ob, wob_ref[...])
    out_ref[...] = x_ref[...] + _dot(m.astype(BF16), wo_ref[...])


def _out_stage(x2d, oa, y, bon, ga, gb, ma, mb, lw):
    n = x2d.shape[0]
    tm = min(256, n)
    assert n % tm == 0
    row = lambda w: pl.BlockSpec((tm, w), lambda i: (i, 0))
    return pl.pallas_call(
        _out_kernel,
        grid=(n // tm,),
        in_specs=[row(D_MODEL), row(512), row(512), row(512), row(512), row(512), row(D_MODEL), row(D_MODEL),
                  _const_spec((8, RW_WIDTH)), _const_spec((RW_WIDTH, RW_WIDTH)),
                  _const_spec((512, D_MODEL)), _const_spec((512, D_MODEL)), _const_spec((D_MODEL, D_MODEL))],
        out_specs=row(D_MODEL),
        out_shape=jax.ShapeDtypeStruct((n, D_MODEL), F32),
        compiler_params=pltpu.CompilerParams(dimension_semantics=("parallel",), vmem_limit_bytes=VMEM_LIMIT),
        name="out_stage",
    )(x2d, oa, y, bon, ga, gb, ma, mb, lw["v512"], lw["seg"], lw["w_out_a"], lw["w_out_b"], lw["w_o"])


def _swap_halves(w):
    half = ROPE // 2
    return jnp.concatenate([w[..., half:], w[..., :half]], axis=-1)


def _prep_weights(norm_w, w_in, q_norm_w, kv_norm_w, w_uq, w_ukv, qn_nope, qn_rope, kn_nope, kn_rope,
                  mu_shift, w0, w2, a0, a2, k_k, k_a, r_k, lnx_w, lnx_b, w_out_a, w_out_b, w_o):
    depth = w_in.shape[0]
    z = lambda *s: jnp.zeros((depth,) + s, F32)
    kpe = w_in[..., 384:416]
    w_lat = jnp.concatenate([w_in[..., :384], kpe, z(D_MODEL, 96), _swap_halves(kpe), z(D_MODEL, 96)],
                            axis=-1).astype(BF16)
    w_in_p = w_in[..., 416:].astype(BF16)
    uq = w_uq.reshape(depth, Q_RANK, HEADS, QK)
    uq_p = jnp.concatenate([uq, z(Q_RANK, HEADS, 32)], axis=-1).reshape(depth, Q_RANK, HEADS * LANES)
    uq_s = jnp.concatenate([z(Q_RANK, HEADS, NOPE), _swap_halves(uq[..., NOPE:]), z(Q_RANK, HEADS, 32)],
                           axis=-1).reshape(depth, Q_RANK, HEADS * LANES)
    ukv = w_ukv.reshape(depth, KV_RANK, HEADS, 2 * NOPE)
    zk = z(KV_RANK, HEADS, NOPE)
    k_part = jnp.concatenate([ukv[..., :NOPE], zk], axis=-1)
    vv = ukv[..., NOPE:]
    odd = (jnp.arange(HEADS) % 2 == 1)[None, None, :, None]
    v_part = jnp.where(odd, jnp.concatenate([zk, vv], axis=-1), jnp.concatenate([vv, zk], axis=-1))
    ukv_p = jnp.swapaxes(jnp.concatenate([k_part, v_part], axis=-1), 1, 2).astype(BF16)
    zg = lambda n: jnp.zeros((depth, n), F32)
    g128 = jnp.stack([
        jnp.concatenate([qn_nope * kn_nope, qn_rope, zg(32)], axis=-1),
        jnp.concatenate([zg(NOPE), _swap_halves(qn_rope), zg(32)], axis=-1),
        jnp.concatenate([kn_rope, zg(96)], axis=-1),
        jnp.concatenate([_swap_halves(kn_rope), zg(96)], axis=-1),
        kv_norm_w,
        zg(LANES), zg(LANES), zg(LANES)], axis=1)
    v512 = jnp.stack([w0, a0, k_k, k_a, r_k.reshape(depth, RW_WIDTH), lnx_w, lnx_b, zg(RW_WIDTH)], axis=1)
    seg_id = jnp.arange(RW_WIDTH) // RW_HEAD
    seg = (seg_id[:, None] == seg_id[None, :]).astype(BF16)
    onecol = jnp.stack([jnp.arange(LANES) == NOPE, jnp.arange(LANES) == 0]).astype(F32)
    place = (jnp.arange(ROPE)[:, None] + NOPE == jnp.arange(LANES)[None, :]).astype(BF16)
    return dict(
        norm_w=norm_w[:, None, :], w_lat=w_lat, w_in=w_in_p, q_norm_w=q_norm_w[:, None, :],
        w_uq=uq_p.astype(BF16), w_uq_sw=uq_s.astype(BF16), w_ukv=ukv_p, g128=g128, mu=mu_shift[:, None, :],
        v512=v512, w2=jnp.concatenate([w2, z(LORA, RW_WIDTH)], axis=1),
        a2=jnp.concatenate([z(LORA, RW_WIDTH), a2], axis=1),
        w_out_a=w_out_a.astype(BF16), w_out_b=w_out_b.astype(BF16), w_o=w_o.astype(BF16),
    ), dict(seg=seg, onecol=onecol, place=place)


def _rope_table(past, seq, batch):
    half = ROPE // 2
    inv = ROPE_THETA ** (-jnp.arange(half, dtype=F32) / half)
    pos = past + jnp.arange(seq, dtype=jnp.int32)
    ang = inv[:, None] * pos.astype(F32)[None, :]
    cs = jnp.concatenate([jnp.cos(ang), jnp.sin(ang)], axis=0)
    sel = jnp.zeros((2 * half, 4 * LANES), F32)
    j = jnp.arange(half)
    for tile, base, row0 in ((0, NOPE, 0), (1, NOPE, half), (2, 0, 0), (3, 0, half)):
        sign_lo = 1.0 if row0 == 0 else -1.0
        sel = sel.at[row0 + j, tile * LANES + base + j].set(sign_lo)
        sel = sel.at[row0 + j, tile * LANES + base + half + j].set(1.0)
    bias = jnp.zeros((1, 4 * LANES), F32).at[0, :NOPE].set(1.0)
    tab = lax.dot_general(cs, sel, _TN, precision=HI, preferred_element_type=F32) + bias
    return jnp.tile(tab, (batch, 1))


def _state_to_pairs(s):
    b = s.shape[0]
    m = jnp.swapaxes(s.astype(F32), -1, -2).reshape(b, HEAD_PAIRS, 2, RW_HEAD, RW_HEAD)
    zero = jnp.zeros_like(m[:, :, 0])
    top = jnp.concatenate([m[:, :, 0], zero], axis=-1)
    bot = jnp.concatenate([zero, m[:, :, 1]], axis=-1)
    return jnp.concatenate([top, bot], axis=-2)


def _pairs_to_state(m):
    b = m.shape[0]
    e = m[:, :, :RW_HEAD, :RW_HEAD]
    o = m[:, :, RW_HEAD:, RW_HEAD:]
    return jnp.swapaxes(jnp.stack([e, o], axis=2).reshape(b, HEADS, RW_HEAD, RW_HEAD), -1, -2)


def _group_layer(x2d, tab, lw, ckv_past2d, kpe_past2d, layer, m0, sh0, tri, *, batch, seq, past):
    q, ckv, kpe, rw, bon, ga, gb, ma, mb, shift = _in_stage(x2d, tab, sh0, lw, seq=seq)
    ckv3 = ckv.reshape(batch, seq, KV_RANK)
    kpe3 = kpe.reshape(batch, seq, ROPE)
    if past > 0:
        oa = _decode_stage(q, ckv_past2d, kpe_past2d, layer * batch, ckv, kpe, lw, batch=batch, seq=seq, past=past)
    else:
        tq = tk = min(512, seq)
        assert seq % tk == 0
        kv_tile = max(t for t in (128, 256, 512, 1024) if seq % t == 0)
        k_att, v_att = _kv_stage(ckv, kpe, lw, tile=kv_tile)
        oa = _attn_stage(q, k_att, v_att, batch=batch, seq=seq, past=0, kv_len=seq, kv_pad=seq, tq=tq, tk=tk)
    y, m1 = _wkv_stage(rw, m0, tri, batch=batch, seq=seq)
    out = _out_stage(x2d, oa, y, bon, ga, gb, ma, mb, lw)
    return out, ckv3, kpe3, _pairs_to_state(m1), shift.reshape(batch, 1, SHIFT_WIDTH)


def kernel(x_prompt, x_sample, cache_ckv, cache_kpe, state_wkv, state_shift, norm_w, w_in, q_norm_w, kv_norm_w,
           w_uq, w_ukv, qn_nope, qn_rope, kn_nope, kn_rope, mu_shift, w0, w2, a0, a2, k_k, k_a, r_k, lnx_w, lnx_b,
           w_out_a, w_out_b, w_o):
    depth = w_in.shape[0]
    bp, tp, _ = x_prompt.shape
    bs, ts, _ = x_sample.shape
    past = cache_ckv.shape[2]
    assert past % CHUNK == 0 and tp % CHUNK == 0 and ts % CHUNK == 0
    stacked, shared = _prep_weights(norm_w, w_in, q_norm_w, kv_norm_w, w_uq, w_ukv, qn_nope, qn_rope, kn_nope,
                                    kn_rope, mu_shift, w0, w2, a0, a2, k_k, k_a, r_k, lnx_w, lnx_b,
                                    w_out_a, w_out_b, w_o)
    tab_p = _rope_table(0, tp, bp)
    tab_s = _rope_table(past, ts, bs)
    tri = (jnp.arange(CHUNK)[:, None] >= jnp.arange(CHUNK)[None, :]).astype(BF16)
    zero_m = jnp.zeros((bp, HEAD_PAIRS, LANES, LANES), F32)
    zero_sh = jnp.zeros((bp, 1, SHIFT_WIDTH), F32)

    ckv_past2d = cache_ckv.reshape(depth * bs * past, KV_RANK)
    kpe_past2d = jnp.swapaxes(cache_kpe, 2, 3).reshape(depth * bs * ROPE, past)
    yp = x_prompt.reshape(bp * tp, D_MODEL)
    ys = x_sample.reshape(bs * ts, D_MODEL)
    outs_p, outs_s = [], []
    for i in range(depth):
        lw = {name: val[i] for name, val in stacked.items()}
        lw.update(shared)
        yp, *rest_p = _group_layer(yp, tab_p, lw, None, None, i, zero_m, zero_sh, tri, batch=bp, seq=tp, past=0)
        outs_p.append(rest_p)
        ys, *rest_s = _group_layer(ys, tab_s, lw, ckv_past2d, kpe_past2d, i, _state_to_pairs(state_wkv[i]),
                                   state_shift[i], tri, batch=bs, seq=ts, past=past)
        outs_s.append(rest_s)
    stack = lambda outs, j: jnp.stack([o[j] for o in outs])
    return (yp.reshape(bp, tp, D_MODEL), ys.reshape(bs, ts, D_MODEL),
            stack(outs_p, 0), stack(outs_p, 1), stack(outs_p, 2), stack(outs_p, 3),
            stack(outs_s, 0), stack(outs_s, 1), stack(outs_s, 2), stack(outs_s, 3))
```
